```python
import jax, jax.numpy as jnp
from jax import lax
import numpy as np

D_MODEL = 4096
BATCH = 2
SEQ = 4096
DEPTH = 2

CTX_LEN = 256
GRID_W = 64
HEAD_DIM = 128
A_HEADS = 8
A_KV_HEADS = 2
WINDOW = 128
B_HEADS = 8
B_KV_HEADS = 2
C_HEADS = 8
RET_DK = 128
RET_DV = 128
RET_CHUNK = 128
ATTN_BLOCK = 128
N_BRANCH = 3
BRANCH_W = 1024
D_FF = 11008
CONV_WIDTH = 3
ROPE_BASE = 10000.0
LN_EPS = 1e-5
RMS_EPS = 1e-6
GN_EPS = 1e-5
NEG_INF = -1e30
DEEPNORM_ALPHA = (2 * DEPTH) ** 0.25
DEEPNORM_BETA = (8 * DEPTH) ** -0.25
IN_SIZES = (A_HEADS * HEAD_DIM, A_KV_HEADS * HEAD_DIM, A_KV_HEADS * HEAD_DIM,
            B_HEADS * HEAD_DIM, B_KV_HEADS * HEAD_DIM, B_KV_HEADS * HEAD_DIM,
            C_HEADS * RET_DK, C_HEADS * RET_DK, C_HEADS * RET_DV, C_HEADS * RET_DV,
            N_BRANCH * D_MODEL)
IN_WIDTH = sum(IN_SIZES)

kernel_name = 'hybrid_parallel_mixer_flow_block'


def layer_norm(x, g, b):
    xf = x.astype(jnp.float32)
    mu = jnp.mean(xf, axis=-1, keepdims=True)
    var = jnp.mean(jnp.square(xf - mu), axis=-1, keepdims=True)
    y = (xf - mu) * lax.rsqrt(var + LN_EPS) * g.astype(jnp.float32) + b.astype(jnp.float32)
    return y.astype(x.dtype)


def rms_norm(x, g):
    xf = x.astype(jnp.float32)
    y = xf * lax.rsqrt(jnp.mean(jnp.square(xf), axis=-1, keepdims=True) + RMS_EPS) * g.astype(jnp.float32)
    return y.astype(x.dtype)


def head_group_norm(o, g):
    mu = jnp.mean(o, axis=-1, keepdims=True)
    var = jnp.mean(jnp.square(o - mu), axis=-1, keepdims=True)
    y = (o - mu) * lax.rsqrt(var + GN_EPS)
    return y.reshape(o.shape[:-2] + (-1,)) * g.astype(jnp.float32)


def split_cols(p, sizes):
    out, start = [], 0
    for n in sizes:
        out.append(p[..., start:start + n])
        start += n
    return out


def split_heads(t, n):
    return t.reshape(t.shape[:-1] + (n, t.shape[-1] // n))


def group_q(t, n_kv):
    return t.reshape(t.shape[:-2] + (n_kv, t.shape[-2] // n_kv, t.shape[-1]))


def flip_t(t):
    return t[:, ::-1]


def rope_half(x, cos, sin):
    half = x.shape[-1] // 2
    x1, x2 = x[..., :half], x[..., half:]
    cos = cos.astype(x.dtype)
    sin = sin.astype(x.dtype)
    return jnp.concatenate([x1 * cos - x2 * sin, x2 * cos + x1 * sin], axis=-1)


def axial_rope(x, cos_r, sin_r, cos_c, sin_c):
    half = x.shape[-1] // 2
    return jnp.concatenate([rope_half(x[..., :half], cos_r, sin_r),
                            rope_half(x[..., half:], cos_c, sin_c)], axis=-1)


def dense_attn(q, k, v, sink=None):
    s = jnp.einsum('bqhgd,bkhd->bhgqk', q, k).astype(jnp.float32) * (q.shape[-1] ** -0.5)
    if sink is not None:
        s_sink = jnp.broadcast_to(sink.astype(jnp.float32)[None, :, :, None, None], s.shape[:-1] + (1,))
        s = jnp.concatenate([s, s_sink], axis=-1)
    p = jax.nn.softmax(s, axis=-1)
    if sink is not None:
        p = p[..., :-1]
    return jnp.einsum('bhgqk,bkhd->bqhgd', p.astype(v.dtype), v)


def blocked_attn(q, k, v):
    B, S, KV, G, d = q.shape
    nb = S // ATTN_BLOCK
    qb = jnp.moveaxis(q.reshape(B, nb, ATTN_BLOCK, KV, G, d), 1, 0)
    ob = lax.map(lambda qi: dense_attn(qi, k, v), qb)
    return jnp.moveaxis(ob, 0, 1).reshape(B, S, KV * G * d)


def window_attn(q, k, v, kc, vc, sink):
    B, S, KV, G, d = q.shape
    nb = S // ATTN_BLOCK
    halo = -(-WINDOW // ATTN_BLOCK)
    pad = halo * ATTN_BLOCK
    qb = q.reshape(B, nb, ATTN_BLOCK, KV, G, d)

    def band(t):
        tp = jnp.pad(t, ((0, 0), (pad, pad), (0, 0), (0, 0))).reshape(B, nb + 2 * halo, ATTN_BLOCK, KV, d)
        return jnp.concatenate([tp[:, j:j + nb] for j in range(2 * halo + 1)], axis=2)

    kb, vb = band(k), band(v)
    n_kb = kb.shape[2]
    qpos = jnp.arange(nb)[:, None] * ATTN_BLOCK + jnp.arange(ATTN_BLOCK)[None, :]
    kpos = (jnp.arange(nb)[:, None] - halo) * ATTN_BLOCK + jnp.arange(n_kb)[None, :]
    valid = ((jnp.abs(qpos[:, :, None] - kpos[:, None, :]) <= WINDOW)
             & (kpos[:, None, :] >= 0) & (kpos[:, None, :] < S))
    scale = d ** -0.5
    s_loc = jnp.einsum('bnqhgd,bnkhd->bnhgqk', qb, kb).astype(jnp.float32) * scale
    s_loc = jnp.where(valid[None, :, None, None], s_loc, NEG_INF)
    s_ctx = jnp.einsum('bnqhgd,bkhd->bnhgqk', qb, kc).astype(jnp.float32) * scale
    s_sink = jnp.broadcast_to(sink.astype(jnp.float32)[None, None, :, :, None, None], s_ctx.shape[:-1] + (1,))
    p = jax.nn.softmax(jnp.concatenate([s_loc, s_ctx, s_sink], axis=-1), axis=-1).astype(v.dtype)
    n_ctx = kc.shape[1]
    o = (jnp.einsum('bnhgqk,bnkhd->bnqhgd', p[..., :n_kb], vb)
         + jnp.einsum('bnhgqk,bkhd->bnqhgd', p[..., n_kb:n_kb + n_ctx], vc))
    return o.reshape(B, S, KV * G * d)


def retention_scan(q, k, v, log_gamma, state0, inclusive):
    B, T, H, _ = q.shape
    dv = v.shape[-1]
    C = RET_CHUNK
    n = T // C

    def chunks(t):
        return jnp.moveaxis(t.reshape(B, n, C, H, t.shape[-1]), 1, 0).transpose(0, 1, 3, 2, 4)

    pos = jnp.arange(C, dtype=jnp.float32)
    diff = pos[:, None] - pos[None, :]
    mask = (diff >= 0) if inclusive else (diff > 0)
    d_intra = jnp.where(mask, jnp.exp(log_gamma[:, None, None] * jnp.maximum(diff, 0.0)), 0.0)
    d_q = jnp.exp(log_gamma[:, None] * (pos + 1.0))[..., None]
    d_k = jnp.exp(log_gamma[:, None] * (C - 1.0 - pos))[..., None]
    d_c = jnp.exp(log_gamma * C)[:, None, None]

    def step(state, qkv):
        qi, ki, vi = qkv
        att = jnp.einsum('bhqd,bhkd->bhqk', qi, ki) * d_intra
        o = (jnp.einsum('bhqk,bhkv->bhqv', att, vi)
             + jnp.einsum('bhqd,bhdv->bhqv', qi, state) * d_q)
        state = state * d_c + jnp.einsum('bhkd,bhkv->bhdv', ki * d_k, vi)
        return state, o

    state, o = lax.scan(step, state0, (chunks(q), chunks(k), chunks(v)))
    o = o.transpose(1, 0, 3, 2, 4).reshape(B, T, H, dv)
    return o, state


def gated_merge(gl, y_a, y_b, y_c, b_gate, w_branch, w_o):
    g = jax.nn.sigmoid((gl + b_gate).astype(jnp.float32)).astype(gl.dtype)
    g = g.reshape(g.shape[:-1] + (N_BRANCH, -1))
    m = (g[..., 0, :] * (y_a @ w_branch[0]) + g[..., 1, :] * (y_b @ w_branch[1])
         + g[..., 2, :] * (y_c @ w_branch[2]))
    return m @ w_o


def hybrid_mixer(h, hc, rope, w_in, b_gate, w_branch, w_o, sink, q_norm_g, k_norm_g,
                 ret_decay, ret_gn_g, with_ctx_out):
    B, S, _ = h.shape
    L = hc.shape[1]
    f32 = jnp.float32
    qa, ka, va, qb, kb, vb, qr, kr, vr, gr, gl = split_cols(h @ w_in, IN_SIZES)
    qa_c, ka_c, va_c, qb_c, kb_c, vb_c, qr_c, kr_c, vr_c, gr_c, gl_c = split_cols(hc @ w_in, IN_SIZES)

    sink_a = sink.reshape(A_KV_HEADS, A_HEADS // A_KV_HEADS)
    qa = group_q(axial_rope(split_heads(qa, A_HEADS), *rope), A_KV_HEADS)
    ka = axial_rope(split_heads(ka, A_KV_HEADS), *rope)
    va = split_heads(va, A_KV_HEADS)
    ka_c, va_c = split_heads(ka_c, A_KV_HEADS), split_heads(va_c, A_KV_HEADS)
    y_a = window_attn(qa, ka, va, ka_c, va_c, sink_a)

    qb = group_q(axial_rope(rms_norm(split_heads(qb, B_HEADS), q_norm_g), *rope), B_KV_HEADS)
    kb = axial_rope(rms_norm(split_heads(kb, B_KV_HEADS), k_norm_g), *rope)
    kb_c = rms_norm(split_heads(kb_c, B_KV_HEADS), k_norm_g)
    vb_c = split_heads(vb_c, B_KV_HEADS)
    k_all = jnp.concatenate([kb_c, kb], axis=1)
    v_all = jnp.concatenate([vb_c, split_heads(vb, B_KV_HEADS)], axis=1)
    y_b = blocked_attn(qb, k_all, v_all)

    log_gamma = -jnp.exp(ret_decay.astype(f32))
    k_scale = RET_DK ** -0.5
    qr = axial_rope(split_heads(qr, C_HEADS), *rope).astype(f32)
    kr = axial_rope(split_heads(kr, C_HEADS), *rope).astype(f32) * k_scale
    vr = split_heads(vr, C_HEADS).astype(f32)
    qr_c = split_heads(qr_c, C_HEADS).astype(f32)
    kr_c = split_heads(kr_c, C_HEADS).astype(f32) * k_scale
    vr_c = split_heads(vr_c, C_HEADS).astype(f32)
    zero = jnp.zeros((B, C_HEADS, RET_DK, RET_DV), f32)
    oc_f, st_f = retention_scan(qr_c, kr_c, vr_c, log_gamma[0], zero, True)
    oc_b, st_b = retention_scan(flip_t(qr_c), flip_t(kr_c), flip_t(vr_c), log_gamma[1], zero, False)
    o_f, _ = retention_scan(qr, kr, vr, log_gamma[0], st_f, True)
    o_b, _ = retention_scan(flip_t(qr), flip_t(kr), flip_t(vr), log_gamma[1], st_b, False)
    y_c = (jax.nn.silu(gr.astype(f32)) * head_group_norm(o_f + flip_t(o_b), ret_gn_g)).astype(h.dtype)

    out = gated_merge(gl, y_a, y_b, y_c, b_gate, w_branch, w_o)
    if not with_ctx_out:
        return out, None
    qa_c = group_q(split_heads(qa_c, A_HEADS), A_KV_HEADS)
    y_a_c = dense_attn(qa_c, ka_c, va_c, sink_a).reshape(B, L, -1)
    qb_c = group_q(rms_norm(split_heads(qb_c, B_HEADS), q_norm_g), B_KV_HEADS)
    y_b_c = dense_attn(qb_c, kb_c, vb_c).reshape(B, L, -1)
    y_c_c = (jax.nn.silu(gr_c.astype(f32)) * head_group_norm(oc_f + flip_t(oc_b), ret_gn_g)).astype(hc.dtype)
    out_c = gated_merge(gl_c, y_a_c, y_b_c, y_c_c, b_gate, w_branch, w_o)
    return out, out_c


def depthwise_conv(z, w, b):
    T = z.shape[1]
    r = CONV_WIDTH // 2
    zp = jnp.pad(z, ((0, 0), (r, CONV_WIDTH - 1 - r), (0, 0)))
    out = b
    for j in range(CONV_WIDTH):
        out = out + zp[:, j:j + T] * w[j]
    return out


def conv_ffn(h, w_up, conv_w, conv_b, w_down):
    gate_pre, up = jnp.split(h @ w_up, 2, axis=-1)
    a = jax.nn.silu(depthwise_conv(gate_pre, conv_w, conv_b)) * up
    return a @ w_down


def setup_inputs(seed: int = 0) -> dict:
    key = jax.random.key(seed)
    ks = jax.random.split(key, 24)
    f32 = jnp.float32

    def nrm(k, shape, scale):
        return jax.random.normal(k, shape, f32) * scale

    heads = jnp.arange(C_HEADS, dtype=f32)
    decay0 = jnp.log(-jnp.log1p(-jnp.exp2(-5.0 - heads)))
    return {
        'x': nrm(ks[0], (BATCH, SEQ, D_MODEL), 1.0),
        'c': nrm(ks[1], (BATCH, D_MODEL), 1.0),
        'ctx': nrm(ks[2], (BATCH, CTX_LEN, D_MODEL), 1.0),
        'c_ctx': nrm(ks[3], (D_MODEL,), 1.0),
        'w_mod': nrm(ks[4], (DEPTH, D_MODEL, 6 * D_MODEL), 0.5 * D_MODEL ** -0.5),
        'b_mod': nrm(ks[5], (DEPTH, 6 * D_MODEL), 0.02),
        'w_in': nrm(ks[6], (DEPTH, D_MODEL, IN_WIDTH), D_MODEL ** -0.5),
        'b_gate': nrm(ks[7], (DEPTH, N_BRANCH * D_MODEL), 0.02),
        'w_branch': nrm(ks[8], (DEPTH, N_BRANCH, BRANCH_W, D_MODEL), BRANCH_W ** -0.5),
        'w_o': nrm(ks[9], (DEPTH, D_MODEL, D_MODEL), DEEPNORM_BETA * D_MODEL ** -0.5),
        'attn_sink': nrm(ks[10], (DEPTH, A_HEADS), 0.5),
        'q_norm_g': 1.0 + nrm(ks[11], (DEPTH, HEAD_DIM), 0.05),
        'k_norm_g': 1.0 + nrm(ks[12], (DEPTH, HEAD_DIM), 0.05),
        'ret_decay': decay0 + nrm(ks[13], (DEPTH, 2, C_HEADS), 0.05),
        'ret_gn_g': 1.0 + nrm(ks[14], (DEPTH, C_HEADS * RET_DV), 0.05),
        'ln1_g': 1.0 + nrm(ks[15], (DEPTH, D_MODEL), 0.05),
        'ln1_b': nrm(ks[16], (DEPTH, D_MODEL), 0.02),
        'w_up': nrm(ks[17], (DEPTH, D_MODEL, 2 * D_FF), D_MODEL ** -0.5),
        'ffn_conv_w': nrm(ks[18], (DEPTH, CONV_WIDTH, D_FF), CONV_WIDTH ** -0.5),
        'ffn_conv_b': nrm(ks[19], (DEPTH, D_FF), 0.02),
        'w_down': nrm(ks[20], (DEPTH, D_FF, D_MODEL), DEEPNORM_BETA * D_FF ** -0.5),
        'ln2_g': 1.0 + nrm(ks[21], (DEPTH, D_MODEL), 0.05),
        'ln2_b': nrm(ks[22], (DEPTH, D_MODEL), 0.02),
    }


def reference(x, c, ctx, c_ctx, w_mod, b_mod, w_in, b_gate, w_branch, w_o, attn_sink, q_norm_g,
              k_norm_g, ret_decay, ret_gn_g, ln1_g, ln1_b, w_up, ffn_conv_w, ffn_conv_b, w_down,
              ln2_g, ln2_b):
    f32 = jnp.float32
    S = x.shape[1]
    ROWS = S // GRID_W
    r_idx, c_idx = jnp.meshgrid(jnp.arange(ROWS, dtype=f32), jnp.arange(GRID_W, dtype=f32), indexing='ij')
    axis_dim = HEAD_DIM // 2
    inv_freq = ROPE_BASE ** (-jnp.arange(0, axis_dim, 2, dtype=f32) / axis_dim)
    ang_r = r_idx.reshape(-1)[:, None] * inv_freq
    ang_c = c_idx.reshape(-1)[:, None] * inv_freq
    rope = (jnp.cos(ang_r)[:, None, :], jnp.sin(ang_r)[:, None, :],
            jnp.cos(ang_c)[:, None, :], jnp.sin(ang_c)[:, None, :])

    lat, cx = x, ctx
    for l in range(DEPTH):
        last = l == DEPTH - 1
        sh1, sc1, g1, sh2, sc2, g2 = [m[:, None, :] for m in
                                      jnp.split(jax.nn.silu(c) @ w_mod[l] + b_mod[l], 6, axis=-1)]
        csh1, csc1, cg1, csh2, csc2, cg2 = jnp.split(jax.nn.silu(c_ctx) @ w_mod[l] + b_mod[l], 6, axis=-1)

        y, y_ctx = hybrid_mixer(lat * (1 + sc1) + sh1, cx * (1 + csc1) + csh1, rope,
                                w_in[l], b_gate[l], w_branch[l], w_o[l], attn_sink[l],
                                q_norm_g[l], k_norm_g[l], ret_decay[l], ret_gn_g[l], not last)
        lat = layer_norm(DEEPNORM_ALPHA * lat + g1 * y, ln1_g[l], ln1_b[l])
        lat = layer_norm(DEEPNORM_ALPHA * lat
                         + g2 * conv_ffn(lat * (1 + sc2) + sh2, w_up[l], ffn_conv_w[l], ffn_conv_b[l], w_down[l]),
                         ln2_g[l], ln2_b[l])
        if not last:
            cx = layer_norm(DEEPNORM_ALPHA * cx + cg1 * y_ctx, ln1_g[l], ln1_b[l])
            cx = layer_norm(DEEPNORM_ALPHA * cx
                            + cg2 * conv_ffn(cx * (1 + csc2) + csh2, w_up[l], ffn_conv_w[l], ffn_conv_b[l], w_down[l]),
                            ln2_g[l], ln2_b[l])
    return lat
```

```python
import functools
import math

import jax
import jax.numpy as jnp
from jax import lax
from jax.experimental import pallas as pl
from jax.experimental.pallas import tpu as pltpu

F32 = jnp.float32
BF16 = jnp.bfloat16

GRID_W = 64
HEAD_DIM = 128
A_HEADS = 8
A_KV_HEADS = 2
B_HEADS = 8
B_KV_HEADS = 2
C_HEADS = 8
WINDOW = 128
RET_CHUNK = 128
N_BRANCH = 3
CONV_WIDTH = 3
ROPE_BASE = 10000.0
LN_EPS = 1e-5
RMS_EPS = 1e-6
GN_EPS = 1e-5
NEG_INF = -1e30

V7X_VMEM_BYTES = 64 * 1024 * 1024
VMEM_LIMIT = V7X_VMEM_BYTES - 8 * 1024 * 1024

_QA, _KA, _VA = 0, 8, 10
_QB, _KB, _VB = 12, 20, 22
_QR, _KR, _VR, _GR = 24, 32, 40, 48
_N_HEADS_QKV = 56
_N_HEADS_ACT = 48


def _params(*sem):
    return pltpu.CompilerParams(dimension_semantics=sem, vmem_limit_bytes=VMEM_LIMIT)


def _tile(n, pref, unit):
    t = min(pref, n) // unit * unit
    while t > unit and n % t:
        t -= unit
    assert t >= unit and n % t == 0, (n, pref, unit)
    return t


def _sigmoid(x):
    return 1.0 / (1.0 + jnp.exp(-x))


def _silu(x):
    return x * _sigmoid(x)


def _mod_kernel(c_ref, w_ref, b_ref, o_ref):
    a = _silu(c_ref[...]).astype(BF16)
    o_ref[...] = jnp.dot(a, w_ref[...].astype(BF16), preferred_element_type=F32) + b_ref[...]


def _modulation(c_rows, w_mod, b_mod):
    depth, d, n = w_mod.shape
    rows = c_rows.shape[0]
    tn = _tile(n, 1024, 128)
    return pl.pallas_call(
        _mod_kernel,
        grid=(depth, n // tn),
        in_specs=[pl.BlockSpec((rows, d), lambda l, j: (0, 0)),
                  pl.BlockSpec((None, d, tn), lambda l, j: (l, 0, j)),
                  pl.BlockSpec((None, 1, tn), lambda l, j: (l, 0, j))],
        out_specs=pl.BlockSpec((None, rows, tn), lambda l, j: (l, 0, j)),
        out_shape=jax.ShapeDtypeStruct((depth, rows, n), F32),
        compiler_params=_params("parallel", "parallel"),
        name="modulation",
    )(c_rows, w_mod, b_mod.reshape(depth, 1, n))


def _modulate_kernel(x_ref, sc_ref, sh_ref, o_ref):
    o_ref[...] = (x_ref[...] * (1.0 + sc_ref[...]) + sh_ref[...]).astype(o_ref.dtype)


def _modulate(x, mod, k_scale, k_shift, group_of):
    m, d = x.shape
    tm = 256
    vec = lambda k: pl.BlockSpec((None, None, 1, d), lambda i: (k, group_of(i * tm), 0, 0))
    return pl.pallas_call(
        _modulate_kernel,
        grid=(m // tm,),
        in_specs=[pl.BlockSpec((tm, d), lambda i: (i, 0)), vec(k_scale), vec(k_shift)],
        out_specs=pl.BlockSpec((tm, d), lambda i: (i, 0)),
        out_shape=jax.ShapeDtypeStruct((m, d), BF16),
        compiler_params=_params("parallel"),
        name="modulate",
    )(x, mod, mod)


def _mm_kernel(a_ref, w_ref, o_ref, wb_ref, *, epilogue):
    @pl.when(pl.program_id(1) == 0)
    def _():
        wb_ref[...] = w_ref[...].astype(BF16)

    acc = jnp.dot(a_ref[...], wb_ref[...], preferred_element_type=F32)
    o_ref[...] = epilogue(acc).astype(o_ref.dtype)


def _mm_bias_sigmoid_kernel(a_ref, w_ref, b_ref, o_ref, wb_ref):
    @pl.when(pl.program_id(1) == 0)
    def _():
        wb_ref[...] = w_ref[...].astype(BF16)

    acc = jnp.dot(a_ref[...], wb_ref[...], preferred_element_type=F32)
    o_ref[...] = _sigmoid(acc + b_ref[...]).astype(o_ref.dtype)


def _mm_plain_kernel(a_ref, w_ref, o_ref):
    o_ref[...] = jnp.dot(a_ref[...], w_ref[...], preferred_element_type=F32).astype(o_ref.dtype)


def _matmul_a_resident(a, w, layer, m, out_dtype, *, tm=512, tn=512, name="matmul"):
    k = a.shape[1]
    n = w.shape[-1]
    tm = _tile(m, tm, 8)
    tn = _tile(n, tn, 128)
    return pl.pallas_call(
        _mm_plain_kernel,
        grid=(m // tm, n // tn),
        in_specs=[pl.BlockSpec((tm, k), lambda i, j: (i, 0)),
                  pl.BlockSpec((None, k, tn), lambda i, j: (layer, 0, j))],
        out_specs=pl.BlockSpec((tm, tn), lambda i, j: (i, j)),
        out_shape=jax.ShapeDtypeStruct((m, n), out_dtype),
        compiler_params=_params("parallel", "arbitrary"),
        name=name,
    )(a, w)


def _matmul(a, w, layer, n0, n, m, out_dtype, *, tm=512, tn=512, bias=None, name="matmul"):
    k = a.shape[1]
    tm = _tile(m, tm, 8)
    tn = _tile(math.gcd(n, n0) if n0 else n, tn, 128)
    j0 = n0 // tn
    in_specs = [pl.BlockSpec((tm, k), lambda j, i: (i, 0)),
                pl.BlockSpec((None, k, tn), lambda j, i: (layer, 0, j + j0))]
    args = [a, w]
    if bias is None:
        body = functools.partial(_mm_kernel, epilogue=lambda x: x)
    else:
        body = _mm_bias_sigmoid_kernel
        in_specs.append(pl.BlockSpec((None, 1, tn), lambda j, i: (layer, 0, j)))
        args.append(bias)
    return pl.pallas_call(
        body,
        grid=(n // tn, m // tm),
        in_specs=in_specs,
        out_specs=pl.BlockSpec((tm, tn), lambda j, i: (i, j)),
        out_shape=jax.ShapeDtypeStruct((m, n), out_dtype),
        scratch_shapes=[pltpu.VMEM((k, tn), BF16)],
        compiler_params=_params("parallel", "arbitrary"),
        name=name,
    )(*args)


def _rope(x, c, sa, sb):
    return x * c + pltpu.roll(x, 96, 1) * sa + pltpu.roll(x, 32, 1) * sb


def _rms(x, g):
    return x * lax.rsqrt(jnp.mean(x * x, axis=-1, keepdims=True) + RMS_EPS) * g


def _prep_kernel(p_ref, c_ref, sa_ref, sb_ref, qg_ref, kg_ref, act_ref, krf_ref, grs_ref):
    c, sa, sb = c_ref[...], sa_ref[...], sb_ref[...]
    qg, kg = qg_ref[...], kg_ref[...]
    k_scale = HEAD_DIM ** -0.5
    for h in range(_N_HEADS_QKV):
        cols = slice(h * HEAD_DIM, (h + 1) * HEAD_DIM)
        x = p_ref[:, cols]
        if h < _KA + A_KV_HEADS:
            y = _rope(x, c, sa, sb)
        elif h < _QB:
            y = x
        elif h < _KB:
            y = _rope(_rms(x, qg), c, sa, sb)
        elif h < _VB:
            y = _rope(_rms(x, kg), c, sa, sb)
        elif h < _QR:
            y = x
        elif h < _KR:
            y = _rope(x, c, sa, sb)
        elif h < _VR:
            y = _rope(x, c, sa, sb) * k_scale
            krf_ref[:, (h - _KR) * HEAD_DIM:(h - _KR + 1) * HEAD_DIM] = y
        elif h < _GR:
            y = x
        else:
            grs_ref[:, (h - _GR) * HEAD_DIM:(h - _GR + 1) * HEAD_DIM] = _silu(x)
            continue
        act_ref[:, cols] = y.astype(BF16)


def _prep(p, rope_c, rope_sa, rope_sb, qg, kg):
    m = p.shape[0]
    tm = 256
    row = lambda w: pl.BlockSpec((tm, w), lambda i: (i, 0))
    vec = pl.BlockSpec((1, HEAD_DIM), lambda i: (0, 0))
    return pl.pallas_call(
        _prep_kernel,
        grid=(m // tm,),
        in_specs=[row(_N_HEADS_QKV * HEAD_DIM), row(HEAD_DIM), row(HEAD_DIM), row(HEAD_DIM), vec, vec],
        out_specs=[row(_N_HEADS_ACT * HEAD_DIM), row(C_HEADS * HEAD_DIM), row(C_HEADS * HEAD_DIM)],
        out_shape=[jax.ShapeDtypeStruct((m, _N_HEADS_ACT * HEAD_DIM), BF16),
                   jax.ShapeDtypeStruct((m, C_HEADS * HEAD_DIM), F32),
                   jax.ShapeDtypeStruct((m, C_HEADS * HEAD_DIM), F32)],
        compiler_params=_params("parallel"),
        name="prep",
    )(p, rope_c, rope_sa, rope_sb, qg, kg)


def _attn_kernel(*refs, n_seg, bands, has_sink, tq, seq, group):
    q_ref = refs[0]
    kv = refs[1:1 + 2 * n_seg]
    pos = 1 + 2 * n_seg
    sink_ref = refs[pos] if has_sink else None
    o_ref = refs[pos + int(has_sink)]
    kvh = pl.program_id(1)
    n = pl.program_id(2)
    scale = HEAD_DIM ** -0.5

    q = jnp.concatenate([q_ref[:, g * HEAD_DIM:(g + 1) * HEAD_DIM] for g in range(group)], axis=0)
    scores = []
    for s in range(n_seg):
        k = kv[2 * s][...]
        sc = lax.dot_general(q, k, (((1,), (1,)), ((), ())), preferred_element_type=F32) * scale
        if bands[s] is not None:
            nk = k.shape[0]
            qpos = n * tq + lax.broadcasted_iota(jnp.int32, (tq, nk), 0)
            kpos = n * tq + bands[s] + lax.broadcasted_iota(jnp.int32, (tq, nk), 1)
            valid = (jnp.abs(qpos - kpos) <= WINDOW) & (kpos >= 0) & (kpos < seq)
            valid = jnp.concatenate([valid] * group, axis=0)
            sc = jnp.where(valid, sc, NEG_INF)
        scores.append(sc)
    mx = functools.reduce(jnp.maximum, [jnp.max(sc, axis=-1, keepdims=True) for sc in scores])
    if has_sink:
        sink = jnp.concatenate(
            [jnp.full((tq, 1), sink_ref[kvh * group + g], F32) for g in range(group)], axis=0)
        mx = jnp.maximum(mx, sink)
    es = [jnp.exp(sc - mx) for sc in scores]
    den = functools.reduce(lambda a, b: a + b, [jnp.sum(e, axis=-1, keepdims=True) for e in es])
    if has_sink:
        den = den + jnp.exp(sink - mx)
    out = None
    for s in range(n_seg):
        p = (es[s] / den).astype(BF16)
        o = jnp.dot(p, kv[2 * s + 1][...], preferred_element_type=F32)
        out = o if out is None else out + o
    for g in range(group):
        o_ref[:, g * HEAD_DIM:(g + 1) * HEAD_DIM] = out[g * tq:(g + 1) * tq].astype(o_ref.dtype)


def _attention(act, segs, *, n_batch, q_rows0, q_len, q_head0, n_kv, group, tq, sink, name):
    gw = group * HEAD_DIM
    nq = q_len // tq
    assert q_rows0 % tq == 0 and q_len % tq == 0
    in_specs = [pl.BlockSpec((tq, gw), lambda b, h, n: (q_rows0 // tq + b * nq + n, q_head0 // group + h))]
    args = [act]
    for rows, row_fn, k0, v0, _ in segs:
        for c0 in (k0, v0):
            in_specs.append(pl.BlockSpec(
                (rows, HEAD_DIM), lambda b, h, n, row_fn=row_fn, c0=c0: (row_fn(b, n), c0 + h)))
            args.append(act)
    if sink is not None:
        in_specs.append(pl.BlockSpec(memory_space=pltpu.SMEM))
        args.append(sink)
    body = functools.partial(
        _attn_kernel, n_seg=len(segs), bands=tuple(s[4] for s in segs), has_sink=sink is not None,
        tq=tq, seq=q_len, group=group)
    return pl.pallas_call(
        body,
        grid=(n_batch, n_kv, nq),
        in_specs=in_specs,
        out_specs=pl.BlockSpec((tq, gw), lambda b, h, n: (b * nq + n, h)),
        out_shape=jax.ShapeDtypeStruct((n_batch * q_len, n_kv * gw), BF16),
        compiler_params=_params("parallel", "parallel", "arbitrary"),
        name=name,
    )(*args)


def _retention_kernel(lg_ref, qf_ref, kf_ref, kff_ref, vf_ref, qb_ref, kb_ref, kbf_ref, vb_ref,
                      of_ref, ob_ref, state_ref):
    c = RET_CHUNK

    @pl.when(pl.program_id(1) == 0)
    def _():
        state_ref[...] = jnp.zeros_like(state_ref)

    row = lax.broadcasted_iota(jnp.int32, (c, c), 0)
    col = lax.broadcasted_iota(jnp.int32, (c, c), 1)
    pos = lax.broadcasted_iota(jnp.int32, (c, 1), 0).astype(F32)
    dirs = ((qf_ref, kf_ref, kff_ref, vf_ref, of_ref), (qb_ref, kb_ref, kbf_ref, vb_ref, ob_ref))
    for d, (q_ref, k_ref, kfull_ref, v_ref, o_ref) in enumerate(dirs):
        diff = (row - col) if d == 0 else (col - row)
        keep = (diff >= 0) if d == 0 else (diff > 0)
        dist = jnp.maximum(diff, 0).astype(F32)
        q_pow = (pos + 1.0) if d == 0 else (c - pos)
        k_pow = (c - 1.0 - pos) if d == 0 else pos
        for h in range(C_HEADS):
            cols = slice(h * HEAD_DIM, (h + 1) * HEAD_DIM)
            lg = lg_ref[d, h]
            d_intra = jnp.where(keep, jnp.exp(lg * dist), 0.0)
            d_q = jnp.exp(lg * q_pow)
            d_k = jnp.exp(lg * k_pow)
            d_c = jnp.exp(jnp.full((1, 1), lg * c, F32))
            q, k, v = q_ref[:, cols], k_ref[:, cols], v_ref[:, cols]
            state = state_ref[d, h]
            att = lax.dot_general(q, k, (((1,), (1,)), ((), ())), preferred_element_type=F32) * d_intra
            o = (jnp.dot(att.astype(BF16), v, preferred_element_type=F32)
                 + jnp.dot(q, state.astype(BF16), preferred_element_type=F32) * d_q)
            o_ref[:, cols] = o
            kd = (kfull_ref[:, cols] * d_k).T.astype(BF16)
            state_ref[d, h] = state * d_c + jnp.dot(kd, v, preferred_element_type=F32)


def _retention(act, krf, log_gamma, *, n_batch, seq, ctx_len):
    c = RET_CHUNK
    w = C_HEADS * HEAD_DIM
    m = act.shape[0]
    n_ctx, n_lat = ctx_len // c, seq // c
    ctx0 = n_batch * seq // c

    def fwd(b, t):
        return jnp.where(t < n_ctx, ctx0 + b * n_ctx + t, b * n_lat + t - n_ctx)

    def bwd(b, t):
        return jnp.where(t < n_ctx, ctx0 + b * n_ctx + (n_ctx - 1 - t), b * n_lat + (n_lat - 1 - (t - n_ctx)))

    def spec(row_fn, col_block):
        return pl.BlockSpec((c, w), lambda b, t: (row_fn(b, t), col_block))

    qc, kc, vc = _QR * HEAD_DIM // w, _KR * HEAD_DIM // w, _VR * HEAD_DIM // w
    return pl.pallas_call(
        _retention_kernel,
        grid=(n_batch, n_ctx + n_lat),
        in_specs=[pl.BlockSpec(memory_space=pltpu.SMEM),
                  spec(fwd, qc), spec(fwd, kc), spec(fwd, 0), spec(fwd, vc),
                  spec(bwd, qc), spec(bwd, kc), spec(bwd, 0), spec(bwd, vc)],
        out_specs=[spec(fwd, 0), spec(bwd, 0)],
        out_shape=[jax.ShapeDtypeStruct((m, w), F32), jax.ShapeDtypeStruct((m, w), F32)],
        scratch_shapes=[pltpu.VMEM((2, C_HEADS, HEAD_DIM, HEAD_DIM), F32)],
        compiler_params=_params("parallel", "arbitrary"),
        name="retention",
    )(log_gamma, act, act, krf, act, act, act, krf, act)


def _gn_gate_kernel(of_ref, ob_ref, grs_ref, g_ref, o_ref):
    for h in range(C_HEADS):
        cols = slice(h * HEAD_DIM, (h + 1) * HEAD_DIM)
        o = of_ref[:, cols] + ob_ref[:, cols]
        mu = jnp.mean(o, axis=-1, keepdims=True)
        var = jnp.mean(jnp.square(o - mu), axis=-1, keepdims=True)
        y = (o - mu) * lax.rsqrt(var + GN_EPS) * g_ref[:, cols]
        o_ref[:, cols] = (grs_ref[:, cols] * y).astype(o_ref.dtype)


def _gn_gate(o_f, o_b, grs, gn_g, layer, m):
    w = o_f.shape[1]
    tm = 256
    row = pl.BlockSpec((tm, w), lambda i: (i, 0))
    return pl.pallas_call(
        _gn_gate_kernel,
        grid=(m // tm,),
        in_specs=[row, row, row, pl.BlockSpec((None, 1, w), lambda i: (layer, 0, 0))],
        out_specs=row,
        out_shape=jax.ShapeDtypeStruct((m, w), BF16),
        compiler_params=_params("parallel"),
        name="gn_gate",
    )(o_f, o_b, grs, gn_g)


def _merge_kernel(ya_ref, yb_ref, yc_ref, w_ref, ga_ref, gb_ref, gc_ref, o_ref, wb_ref):
    @pl.when(pl.program_id(1) == 0)
    def _():
        wb_ref[...] = w_ref[...].astype(BF16)

    out = None
    for i, (y_ref, g_ref) in enumerate(((ya_ref, ga_ref), (yb_ref, gb_ref), (yc_ref, gc_ref))):
        t = g_ref[...].astype(F32) * jnp.dot(y_ref[...], wb_ref[i], preferred_element_type=F32)
        out = t if out is None else out + t
    o_ref[...] = out.astype(o_ref.dtype)


def _merge(y_a, y_b, y_c, gates, w_branch, layer, m):
    d = w_branch.shape[-1]
    bw = w_branch.shape[-2]
    tm = _tile(m, 512, 8)
    tn = _tile(d, 512, 128)
    nj = d // tn
    y_spec = pl.BlockSpec((tm, bw), lambda j, i: (i, 0))
    g_spec = lambda br: pl.BlockSpec((tm, tn), lambda j, i: (i, br * nj + j))
    return pl.pallas_call(
        _merge_kernel,
        grid=(nj, m // tm),
        in_specs=[y_spec, y_spec, y_spec,
                  pl.BlockSpec((None, N_BRANCH, bw, tn), lambda j, i: (layer, 0, 0, j)),
                  g_spec(0), g_spec(1), g_spec(2)],
        out_specs=pl.BlockSpec((tm, tn), lambda j, i: (i, j)),
        out_shape=jax.ShapeDtypeStruct((m, d), BF16),
        scratch_shapes=[pltpu.VMEM((N_BRANCH, bw, tn), BF16)],
        compiler_params=_params("parallel", "arbitrary"),
        name="merge",
    )(y_a, y_b, y_c, w_branch, gates, gates, gates)


def _ln_kernel(lat_ref, y_ref, gate_ref, lng_ref, lnb_ref, sc_ref, sh_ref, lat_out, h_out, *, alpha):
    z = alpha * lat_ref[...] + gate_ref[...] * y_ref[...]
    mu = jnp.mean(z, axis=-1, keepdims=True)
    var = jnp.mean(jnp.square(z - mu), axis=-1, keepdims=True)
    out = (z - mu) * lax.rsqrt(var + LN_EPS) * lng_ref[...] + lnb_ref[...]
    lat_out[...] = out
    h_out[...] = (out * (1.0 + sc_ref[...]) + sh_ref[...]).astype(h_out.dtype)


def _post_ln(lat, y, mod, k_gate, mod_next, k_scale, k_shift, ln_g, ln_b, layer, m, group_of, alpha):
    d = lat.shape[1]
    tm = 256
    row = pl.BlockSpec((tm, d), lambda i: (i, 0))
    vec = lambda k: pl.BlockSpec((None, None, 1, d), lambda i: (k, group_of(i * tm), 0, 0))
    par = pl.BlockSpec((None, 1, d), lambda i: (layer, 0, 0))
    return pl.pallas_call(
        functools.partial(_ln_kernel, alpha=alpha),
        grid=(m // tm,),
        in_specs=[row, row, vec(k_gate), par, par, vec(k_scale), vec(k_shift)],
        out_specs=[row, row],
        out_shape=[jax.ShapeDtypeStruct((m, d), F32), jax.ShapeDtypeStruct((m, d), BF16)],
        compiler_params=_params("parallel"),
        name="post_ln",
    )(lat, y, mod, ln_g, ln_b, mod_next, mod_next)


def _conv_kernel(g_ref, prev_ref, next_ref, up_ref, w_ref, b_ref, o_ref, *, tm, lat_rows, seq, ctx_len):
    start = pl.program_id(0) * tm
    in_lat = start < lat_rows
    pos = jnp.where(in_lat, start % seq, (start - lat_rows) % ctx_len)
    length = jnp.where(in_lat, seq, ctx_len)
    first = pos == 0
    last = pos + tm == length
    g = g_ref[...]
    rows = lax.broadcasted_iota(jnp.int32, g.shape, 0)
    prev_row = jnp.where(first, 0.0, prev_ref[7:8, :])
    next_row = jnp.where(last, 0.0, next_ref[0:1, :])
    g_m1 = jnp.where(rows == 0, prev_row, pltpu.roll(g, 1, 0))
    g_p1 = jnp.where(rows == tm - 1, next_row, pltpu.roll(g, tm - 1, 0))
    conv = b_ref[...] + g_m1 * w_ref[0:1, :] + g * w_ref[1:2, :] + g_p1 * w_ref[2:3, :]
    o_ref[...] = (_silu(conv) * up_ref[...]).astype(o_ref.dtype)


def _conv_gate(z, conv_w, conv_b, layer, m, *, lat_rows, seq, ctx_len):
    f = z.shape[1] // 2
    tm = _tile(ctx_len, 128, 8)
    tn = _tile(f, 5632, 128)
    nj = f // tn
    n8 = z.shape[0] // 8
    r8 = tm // 8
    body = functools.partial(_conv_kernel, tm=tm, lat_rows=lat_rows, seq=seq, ctx_len=ctx_len)
    return pl.pallas_call(
        body,
        grid=(m // tm, nj),
        in_specs=[pl.BlockSpec((tm, tn), lambda i, j: (i, j)),
                  pl.BlockSpec((8, tn), lambda i, j: (jnp.maximum(i * r8 - 1, 0), j)),
                  pl.BlockSpec((8, tn), lambda i, j: (jnp.minimum((i + 1) * r8, n8 - 1), j)),
                  pl.BlockSpec((tm, tn), lambda i, j: (i, nj + j)),
                  pl.BlockSpec((None, CONV_WIDTH, tn), lambda i, j: (layer, 0, j)),
                  pl.BlockSpec((None, 1, tn), lambda i, j: (layer, 0, j))],
        out_specs=pl.BlockSpec((tm, tn), lambda i, j: (i, j)),
        out_shape=jax.ShapeDtypeStruct((m, f), BF16),
        compiler_params=_params("parallel", "parallel"),
        name="conv_gate",
    )(z, z, z, z, conv_w, conv_b)


def _rope_tables(seq, n_batch, n_ctx_rows):
    rows = seq // GRID_W
    r_idx, c_idx = jnp.meshgrid(jnp.arange(rows, dtype=F32), jnp.arange(GRID_W, dtype=F32), indexing='ij')
    axis_dim = HEAD_DIM // 2
    inv_freq = ROPE_BASE ** (-jnp.arange(0, axis_dim, 2, dtype=F32) / axis_dim)
    ang_r = r_idx.reshape(-1)[:, None] * inv_freq
    ang_c = c_idx.reshape(-1)[:, None] * inv_freq
    cr, sr, cc, sn = jnp.cos(ang_r), jnp.sin(ang_r), jnp.cos(ang_c), jnp.sin(ang_c)
    zero = jnp.zeros_like(sr)
    c = jnp.concatenate([cr, cr, cc, cc], axis=-1)
    sa = jnp.concatenate([-sr, zero, -sn, zero], axis=-1)
    sb = jnp.concatenate([zero, sr, zero, sn], axis=-1)

    def full(t, fill):
        return jnp.concatenate([jnp.tile(t, (n_batch, 1)), jnp.full((n_ctx_rows, HEAD_DIM), fill, F32)], axis=0)

    return full(c, 1.0), full(sa, 0.0), full(sb, 0.0)


def kernel(x, c, ctx, c_ctx, w_mod, b_mod, w_in, b_gate, w_branch, w_o, attn_sink, q_norm_g, k_norm_g,
           ret_decay, ret_gn_g, ln1_g, ln1_b, w_up, ffn_conv_w, ffn_conv_b, w_down, ln2_g, ln2_b):
    n_batch, seq, d = x.shape
    ctx_len = ctx.shape[1]
    depth = w_mod.shape[0]
    d_ff = w_down.shape[1]
    lat_rows = n_batch * seq
    m_all = lat_rows + n_batch * ctx_len
    alpha = (2 * depth) ** 0.25
    qkv_w = _N_HEADS_QKV * HEAD_DIM
    assert seq % 256 == 0 and ctx_len % 256 == 0 and lat_rows % ctx_len == 0

    def group_of(row_start):
        return jnp.minimum(row_start // seq, n_batch)

    c_rows = jnp.concatenate([c, c_ctx[None], jnp.zeros((8 - n_batch - 1, d), F32)], axis=0)
    mod = _modulation(c_rows, w_mod, b_mod)
    mod = mod.reshape(depth, 8, 6, d).transpose(0, 2, 1, 3)[:, :, :, None, :]
    SH1, SC1, G1, SH2, SC2, G2 = range(6)

    rope_c, rope_sa, rope_sb = _rope_tables(seq, n_batch, n_batch * ctx_len)
    log_gamma = -jnp.exp(ret_decay.astype(F32))
    b_gate3 = b_gate.reshape(depth, 1, -1)
    gn_g3 = ret_gn_g.reshape(depth, 1, -1)
    ln1_g3, ln1_b3 = ln1_g.reshape(depth, 1, d), ln1_b.reshape(depth, 1, d)
    ln2_g3, ln2_b3 = ln2_g.reshape(depth, 1, d), ln2_b.reshape(depth, 1, d)
    conv_b3 = ffn_conv_b.reshape(depth, 1, d_ff)
    w_down_bf = w_down.astype(BF16)

    lat = jnp.concatenate([x.reshape(lat_rows, d), ctx.reshape(n_batch * ctx_len, d)], axis=0)
    h = _modulate(lat, mod[0], SC1, SH1, group_of)

    for l in range(depth):
        last = l == depth - 1
        m = lat_rows if last else m_all

        p = _matmul(h, w_in, l, 0, qkv_w, m_all, F32, name="in_proj_qkv")
        gates = _matmul(h, w_in, l, qkv_w, N_BRANCH * d, m, BF16, bias=b_gate3, name="in_proj_gate")
        act, krf, grs = _prep(p, rope_c, rope_sa, rope_sb, q_norm_g[l][None], k_norm_g[l][None])

        tq = 256
        r128 = tq // 128
        lat128 = seq // 128
        ctx_seg = lambda k0, v0: (ctx_len, lambda b, n: lat_rows // ctx_len + b, k0, v0, None)
        segs_a = [
            (128, lambda b, n: b * lat128 + jnp.maximum(n * r128 - 1, 0), _KA, _VA, -128),
            (tq, lambda b, n: b * (seq // tq) + n, _KA, _VA, 0),
            (128, lambda b, n: b * lat128 + jnp.minimum((n + 1) * r128, lat128 - 1), _KA, _VA, tq),
            ctx_seg(_KA, _VA),
        ]
        attn = functools.partial(_attention, act, n_batch=n_batch)
        y_a = attn(segs_a, q_rows0=0, q_len=seq, q_head0=_QA, n_kv=A_KV_HEADS,
                   group=A_HEADS // A_KV_HEADS, tq=tq, sink=attn_sink[l], name="attn_window")
        segs_b = [ctx_seg(_KB, _VB), (seq, lambda b, n: b, _KB, _VB, None)]
        y_b = attn(segs_b, q_rows0=0, q_len=seq, q_head0=_QB, n_kv=B_KV_HEADS,
                   group=B_HEADS // B_KV_HEADS, tq=128, sink=None, name="attn_dense")
        if not last:
            y_a_c = attn([ctx_seg(_KA, _VA)], q_rows0=lat_rows, q_len=ctx_len, q_head0=_QA,
                         n_kv=A_KV_HEADS, group=A_HEADS // A_KV_HEADS, tq=ctx_len, sink=attn_sink[l],
                         name="attn_ctx_a")
            y_b_c = attn([ctx_seg(_KB, _VB)], q_rows0=lat_rows, q_len=ctx_len, q_head0=_QB,
                         n_kv=B_KV_HEADS, group=B_HEADS // B_KV_HEADS, tq=ctx_len, sink=None,
                         name="attn_ctx_b")
            y_a = jnp.concatenate([y_a, y_a_c], axis=0)
            y_b = jnp.concatenate([y_b, y_b_c], axis=0)

        o_f, o_b = _retention(act, krf, log_gamma[l], n_batch=n_batch, seq=seq, ctx_len=ctx_len)
        y_c = _gn_gate(o_f, o_b, grs, gn_g3, l, m)

        merged = _merge(y_a, y_b, y_c, gates, w_branch, l, m)
        y = _matmul(merged, w_o, l, 0, d, m, F32, name="out_proj")
        lat, h2 = _post_ln(lat, y, mod[l], G1, mod[l], SC2, SH2, ln1_g3, ln1_b3, l, m, group_of, alpha)

        z = _matmul(h2, w_up, l, 0, 2 * d_ff, m, F32, name="up_proj")
        a = _conv_gate(z, ffn_conv_w, conv_b3, l, m, lat_rows=lat_rows, seq=seq, ctx_len=ctx_len)
        y2 = _matmul_a_resident(a, w_down_bf, l, m, F32, name="down_proj")
        nxt = mod[l] if last else mod[l + 1]
        lat, h = _post_ln(lat, y2, mod[l], G2, nxt, SC1, SH1, ln2_g3, ln2_b3, l, m, group_of, alpha)

    return lat.reshape(n_batch, seq, d)
```

```python
import functools
import math

import jax
import jax.numpy as jnp
from jax import lax
from jax.experimental import pallas as pl
from jax.experimental.pallas import tpu as pltpu

F32 = jnp.float32
BF16 = jnp.bfloat16

GRID_W = 64
HEAD_DIM = 128
A_HEADS = 8
A_KV_HEADS = 2
B_HEADS = 8
B_KV_HEADS = 2
C_HEADS = 8
WINDOW = 128
RET_CHUNK = 128
N_BRANCH = 3
CONV_WIDTH = 3
ROPE_BASE = 10000.0
LN_EPS = 1e-5
RMS_EPS = 1e-6
GN_EPS = 1e-5
NEG_INF = -1e30

V7X_VMEM_BYTES = 64 * 1024 * 1024
VMEM_LIMIT = V7X_VMEM_BYTES - 8 * 1024 * 1024

_QA, _KA, _VA = 0, 8, 10
_QB, _KB, _VB = 12, 20, 22
_QR, _KR, _VR, _GR = 24, 32, 40, 48
_N_HEADS_QKV = 56
_N_HEADS_ACT = 48


def _params(*sem):
    return pltpu.CompilerParams(dimension_semantics=sem, vmem_limit_bytes=VMEM_LIMIT)


def _tile(n, pref, unit):
    t = min(pref, n) // unit * unit
    while t > unit and n % t:
        t -= unit
    assert t >= unit and n % t == 0, (n, pref, unit)
    return t


def _sigmoid(x):
    return 1.0 / (1.0 + jnp.exp(-x))


def _silu(x):
    return x * _sigmoid(x)


def _mod_kernel(c_ref, w_ref, b_ref, o_ref):
    a = _silu(c_ref[...]).astype(BF16)
    o_ref[...] = jnp.dot(a, w_ref[...].astype(BF16), preferred_element_type=F32) + b_ref[...]


def _modulation(c_rows, w_mod, b_mod):
    depth, d, n = w_mod.shape
    rows = c_rows.shape[0]
    tn = _tile(n, 1024, 128)
    return pl.pallas_call(
        _mod_kernel,
        grid=(depth, n // tn),
        in_specs=[pl.BlockSpec((rows, d), lambda l, j: (0, 0)),
                  pl.BlockSpec((None, d, tn), lambda l, j: (l, 0, j)),
                  pl.BlockSpec((None, 1, tn), lambda l, j: (l, 0, j))],
        out_specs=pl.BlockSpec((None, rows, tn), lambda l, j: (l, 0, j)),
        out_shape=jax.ShapeDtypeStruct((depth, rows, n), F32),
        compiler_params=_params("parallel", "parallel"),
        name="modulation",
    )(c_rows, w_mod, b_mod.reshape(depth, 1, n))


def _modulate_kernel(x_ref, sc_ref, sh_ref, o_ref):
    o_ref[...] = (x_ref[...] * (1.0 + sc_ref[...]) + sh_ref[...]).astype(o_ref.dtype)


def _modulate(x, mod, k_scale, k_shift, group_of):
    m, d = x.shape
    tm = 256
    vec = lambda k: pl.BlockSpec((None, None, 1, d), lambda i: (k, group_of(i * tm), 0, 0))
    return pl.pallas_call(
        _modulate_kernel,
        grid=(m // tm,),
        in_specs=[pl.BlockSpec((tm, d), lambda i: (i, 0)), vec(k_scale), vec(k_shift)],
        out_specs=pl.BlockSpec((tm, d), lambda i: (i, 0)),
        out_shape=jax.ShapeDtypeStruct((m, d), BF16),
        compiler_params=_params("parallel"),
        name="modulate",
    )(x, mod, mod)


def _mm_kernel(a_ref, w_ref, *rest, kc, n_col_tiles, has_bias):
    b_ref = rest[0] if has_bias else None
    o_ref, wb_ref = rest[-2], rest[-1]
    j, i = pl.program_id(0), pl.program_id(1)

    @pl.when(j < n_col_tiles)
    def _():
        wb_ref[j % 2, pl.ds(pl.multiple_of(i * kc, kc), kc), :] = w_ref[...].astype(BF16)

    @pl.when(j > 0)
    def _():
        acc = jnp.dot(a_ref[...], wb_ref[(j + 1) % 2], preferred_element_type=F32)
        if has_bias:
            acc = _sigmoid(acc + b_ref[...])
        o_ref[...] = acc.astype(o_ref.dtype)


def _mm_plain_kernel(a_ref, w_ref, o_ref):
    o_ref[...] = jnp.dot(a_ref[...], w_ref[...], preferred_element_type=F32).astype(o_ref.dtype)


def _matmul_a_resident(a, w, layer, m, out_dtype, *, tm=512, tn=512, name="matmul"):
    k = a.shape[1]
    n = w.shape[-1]
    tm = _tile(m, tm, 8)
    tn = _tile(n, tn, 128)
    return pl.pallas_call(
        _mm_plain_kernel,
        grid=(m // tm, n // tn),
        in_specs=[pl.BlockSpec((tm, k), lambda i, j: (i, 0)),
                  pl.BlockSpec((None, k, tn), lambda i, j: (layer, 0, j))],
        out_specs=pl.BlockSpec((tm, tn), lambda i, j: (i, j)),
        out_shape=jax.ShapeDtypeStruct((m, n), out_dtype),
        compiler_params=_params("parallel", "arbitrary"),
        name=name,
    )(a, w)


_MM_ROW_TILES = 8


def _matmul(a, w, layer, n0, n, m, out_dtype, *, tn=1024, bias=None, name="matmul"):
    k = a.shape[1]
    ni = _MM_ROW_TILES
    tm, kc = m // ni, k // ni
    assert m % ni == 0 and tm % 8 == 0 and k % ni == 0 and kc % 16 == 0
    tn = _tile(math.gcd(n, n0) if n0 else n, tn, 128)
    j0, nj = n0 // tn, n // tn
    row = lambda j, i: jnp.where(j == 0, 0, i)
    col = lambda j: jnp.maximum(j - 1, 0)
    in_specs = [pl.BlockSpec((tm, k), lambda j, i: (row(j, i), 0)),
                pl.BlockSpec((None, kc, tn),
                             lambda j, i: (layer, jnp.where(j == nj, ni - 1, i), jnp.minimum(j, nj - 1) + j0))]
    args = [a, w]
    if bias is not None:
        in_specs.append(pl.BlockSpec((None, 1, tn), lambda j, i: (layer, 0, col(j))))
        args.append(bias)
    return pl.pallas_call(
        functools.partial(_mm_kernel, kc=kc, n_col_tiles=nj, has_bias=bias is not None),
        grid=(nj + 1, ni),
        in_specs=in_specs,
        out_specs=pl.BlockSpec((tm, tn), lambda j, i: (row(j, i), col(j))),
        out_shape=jax.ShapeDtypeStruct((m, n), out_dtype),
        scratch_shapes=[pltpu.VMEM((2, k, tn), BF16)],
        compiler_params=_params("arbitrary", "arbitrary"),
        name=name,
    )(*args)


def _rope(x, c, sa, sb):
    return x * c + pltpu.roll(x, 96, 1) * sa + pltpu.roll(x, 32, 1) * sb


def _rms(x, g):
    return x * lax.rsqrt(jnp.mean(x * x, axis=-1, keepdims=True) + RMS_EPS) * g


def _prep_kernel(p_ref, c_ref, sa_ref, sb_ref, qg_ref, kg_ref, act_ref, krf_ref, grs_ref):
    c, sa, sb = c_ref[...], sa_ref[...], sb_ref[...]
    qg, kg = qg_ref[...], kg_ref[...]
    k_scale = HEAD_DIM ** -0.5
    for h in range(_N_HEADS_QKV):
        cols = slice(h * HEAD_DIM, (h + 1) * HEAD_DIM)
        x = p_ref[:, cols]
        if h < _KA + A_KV_HEADS:
            y = _rope(x, c, sa, sb)
        elif h < _QB:
            y = x
        elif h < _KB:
            y = _rope(_rms(x, qg), c, sa, sb)
        elif h < _VB:
            y = _rope(_rms(x, kg), c, sa, sb)
        elif h < _QR:
            y = x
        elif h < _KR:
            y = _rope(x, c, sa, sb)
        elif h < _VR:
            y = _rope(x, c, sa, sb) * k_scale
            krf_ref[:, (h - _KR) * HEAD_DIM:(h - _KR + 1) * HEAD_DIM] = y
        elif h < _GR:
            y = x
        else:
            grs_ref[:, (h - _GR) * HEAD_DIM:(h - _GR + 1) * HEAD_DIM] = _silu(x)
            continue
        act_ref[:, cols] = y.astype(BF16)


def _prep(p, rope_c, rope_sa, rope_sb, qg, kg):
    m = p.shape[0]
    tm = 256
    row = lambda w: pl.BlockSpec((tm, w), lambda i: (i, 0))
    vec = pl.BlockSpec((1, HEAD_DIM), lambda i: (0, 0))
    return pl.pallas_call(
        _prep_kernel,
        grid=(m // tm,),
        in_specs=[row(_N_HEADS_QKV * HEAD_DIM), row(HEAD_DIM), row(HEAD_DIM), row(HEAD_DIM), vec, vec],
        out_specs=[row(_N_HEADS_ACT * HEAD_DIM), row(C_HEADS * HEAD_DIM), row(C_HEADS * HEAD_DIM)],
        out_shape=[jax.ShapeDtypeStruct((m, _N_HEADS_ACT * HEAD_DIM), BF16),
                   jax.ShapeDtypeStruct((m, C_HEADS * HEAD_DIM), F32),
                   jax.ShapeDtypeStruct((m, C_HEADS * HEAD_DIM), F32)],
        compiler_params=_params("parallel"),
        name="prep",
    )(p, rope_c, rope_sa, rope_sb, qg, kg)


def _attn_kernel(*refs, n_seg, bands, has_sink, tq, seq, group, kchunk):
    q_ref = refs[0]
    kv = refs[1:1 + 2 * n_seg]
    pos = 1 + 2 * n_seg
    sink_ref = refs[pos] if has_sink else None
    o_ref = refs[pos + int(has_sink)]
    kvh = pl.program_id(1)
    n = pl.program_id(2)
    scale = HEAD_DIM ** -0.5
    log2e = math.log2(math.e)

    q = jnp.concatenate([q_ref[:, g * HEAD_DIM:(g + 1) * HEAD_DIM] for g in range(group)], axis=0)
    def raw_scores(s, c0, c1):
        sc = lax.dot_general(q, kv[2 * s][c0:c1, :], (((1,), (1,)), ((), ())), preferred_element_type=F32)
        if bands[s] is not None:
            qpos = n * tq + lax.broadcasted_iota(jnp.int32, (tq, c1 - c0), 0)
            kpos = n * tq + (bands[s] + c0) + lax.broadcasted_iota(jnp.int32, (tq, c1 - c0), 1)
            valid = (jnp.abs(qpos - kpos) <= WINDOW) & (kpos >= 0) & (kpos < seq)
            sc = jnp.where(jnp.concatenate([valid] * group, axis=0), sc, NEG_INF)
        return sc

    chunks = [(s, c0, min(c0 + kchunk, kv[2 * s].shape[0]))
              for s in range(n_seg) for c0 in range(0, kv[2 * s].shape[0], kchunk)]
    mx = functools.reduce(jnp.maximum, [jnp.max(raw_scores(*ch), axis=-1, keepdims=True) for ch in chunks])
    mx = mx * scale
    if has_sink:
        sink = jnp.concatenate(
            [jnp.full((tq, 1), sink_ref[kvh * group + g], F32) for g in range(group)], axis=0)
        mx = jnp.maximum(mx, sink)
    mx2 = mx * log2e
    den, out = None, None
    for s, c0, c1 in chunks:
        e = jnp.exp2(raw_scores(s, c0, c1) * (scale * log2e) - mx2)
        r = jnp.sum(e, axis=-1, keepdims=True)
        o = jnp.dot(e.astype(BF16), kv[2 * s + 1][c0:c1, :], preferred_element_type=F32)
        den = r if den is None else den + r
        out = o if out is None else out + o
    if has_sink:
        den = den + jnp.exp2(sink * log2e - mx2)
    out = out / den
    for g in range(group):
        o_ref[:, g * HEAD_DIM:(g + 1) * HEAD_DIM] = out[g * tq:(g + 1) * tq].astype(o_ref.dtype)


def _attention(act, segs, *, n_batch, q_rows0, q_len, q_head0, n_kv, group, tq, sink, name):
    gw = group * HEAD_DIM
    nq = q_len // tq
    assert q_rows0 % tq == 0 and q_len % tq == 0
    in_specs = [pl.BlockSpec((tq, gw), lambda b, h, n: (q_rows0 // tq + b * nq + n, q_head0 // group + h))]
    args = [act]
    for rows, row_fn, k0, v0, _ in segs:
        for c0 in (k0, v0):
            in_specs.append(pl.BlockSpec(
                (rows, HEAD_DIM), lambda b, h, n, row_fn=row_fn, c0=c0: (row_fn(b, n), c0 + h)))
            args.append(act)
    if sink is not None:
        in_specs.append(pl.BlockSpec(memory_space=pltpu.SMEM))
        args.append(sink)
    body = functools.partial(
        _attn_kernel, n_seg=len(segs), bands=tuple(s[4] for s in segs), has_sink=sink is not None,
        tq=tq, seq=q_len, group=group, kchunk=512)
    return pl.pallas_call(
        body,
        grid=(n_batch, n_kv, nq),
        in_specs=in_specs,
        out_specs=pl.BlockSpec((tq, gw), lambda b, h, n: (b * nq + n, h)),
        out_shape=jax.ShapeDtypeStruct((n_batch * q_len, n_kv * gw), BF16),
        compiler_params=_params("parallel", "parallel", "arbitrary"),
        name=name,
    )(*args)


def _retention_kernel(lg_ref, qf_ref, kf_ref, kff_ref, vf_ref, qb_ref, kb_ref, kbf_ref, vb_ref,
                      of_ref, ob_ref, state_ref):
    c = RET_CHUNK

    @pl.when(pl.program_id(1) == 0)
    def _():
        state_ref[...] = jnp.zeros_like(state_ref)

    row = lax.broadcasted_iota(jnp.int32, (c, c), 0)
    col = lax.broadcasted_iota(jnp.int32, (c, c), 1)
    pos = lax.broadcasted_iota(jnp.int32, (c, 1), 0).astype(F32)
    dirs = ((qf_ref, kf_ref, kff_ref, vf_ref, of_ref), (qb_ref, kb_ref, kbf_ref, vb_ref, ob_ref))
    for d, (q_ref, k_ref, kfull_ref, v_ref, o_ref) in enumerate(dirs):
        diff = (row - col) if d == 0 else (col - row)
        keep = (diff >= 0) if d == 0 else (diff > 0)
        dist = jnp.maximum(diff, 0).astype(F32)
        q_pow = (pos + 1.0) if d == 0 else (c - pos)
        k_pow = (c - 1.0 - pos) if d == 0 else pos
        for h in range(C_HEADS):
            cols = slice(h * HEAD_DIM, (h + 1) * HEAD_DIM)
            lg = lg_ref[d, h]
            d_intra = jnp.where(keep, jnp.exp(lg * dist), 0.0)
            d_q = jnp.exp(lg * q_pow)
            d_k = jnp.exp(lg * k_pow)
            d_c = jnp.exp(jnp.full((1, 1), lg * c, F32))
            q, k, v = q_ref[:, cols], k_ref[:, cols], v_ref[:, cols]
            state = state_ref[d, h]
            att = lax.dot_general(q, k, (((1,), (1,)), ((), ())), preferred_element_type=F32) * d_intra
            o = (jnp.dot(att.astype(BF16), v, preferred_element_type=F32)
                 + jnp.dot(q, state.astype(BF16), preferred_element_type=F32) * d_q)
            o_ref[:, cols] = o
            kd = (kfull_ref[:, cols] * d_k).T.astype(BF16)
            state_ref[d, h] = state * d_c + jnp.dot(kd, v, preferred_element_type=F32)


def _retention(act, krf, log_gamma, *, n_batch, seq, ctx_len):
    c = RET_CHUNK
    w = C_HEADS * HEAD_DIM
    m = act.shape[0]
    n_ctx, n_lat = ctx_len // c, seq // c
    ctx0 = n_batch * seq // c

    def fwd(b, t):
        return jnp.where(t < n_ctx, ctx0 + b * n_ctx + t, b * n_lat + t - n_ctx)

    def bwd(b, t):
        return jnp.where(t < n_ctx, ctx0 + b * n_ctx + (n_ctx - 1 - t), b * n_lat + (n_lat - 1 - (t - n_ctx)))

    def spec(row_fn, col_block):
        return pl.BlockSpec((c, w), lambda b, t: (row_fn(b, t), col_block))

    qc, kc, vc = _QR * HEAD_DIM // w, _KR * HEAD_DIM // w, _VR * HEAD_DIM // w
    return pl.pallas_call(
        _retention_kernel,
        grid=(n_batch, n_ctx + n_lat),
        in_specs=[pl.BlockSpec(memory_space=pltpu.SMEM),
                  spec(fwd, qc), spec(fwd, kc), spec(fwd, 0), spec(fwd, vc),
                  spec(bwd, qc), spec(bwd, kc), spec(bwd, 0), spec(bwd, vc)],
        out_specs=[spec(fwd, 0), spec(bwd, 0)],
        out_shape=[jax.ShapeDtypeStruct((m, w), F32), jax.ShapeDtypeStruct((m, w), F32)],
        scratch_shapes=[pltpu.VMEM((2, C_HEADS, HEAD_DIM, HEAD_DIM), F32)],
        compiler_params=_params("parallel", "arbitrary"),
        name="retention",
    )(log_gamma, act, act, krf, act, act, act, krf, act)


def _gn_gate_kernel(of_ref, ob_ref, grs_ref, g_ref, o_ref):
    for h in range(C_HEADS):
        cols = slice(h * HEAD_DIM, (h + 1) * HEAD_DIM)
        o = of_ref[:, cols] + ob_ref[:, cols]
        mu = jnp.mean(o, axis=-1, keepdims=True)
        var = jnp.mean(jnp.square(o - mu), axis=-1, keepdims=True)
        y = (o - mu) * lax.rsqrt(var + GN_EPS) * g_ref[:, cols]
        o_ref[:, cols] = (grs_ref[:, cols] * y).astype(o_ref.dtype)


def _gn_gate(o_f, o_b, grs, gn_g, layer, m):
    w = o_f.shape[1]
    tm = 256
    row = pl.BlockSpec((tm, w), lambda i: (i, 0))
    return pl.pallas_call(
        _gn_gate_kernel,
        grid=(m // tm,),
        in_specs=[row, row, row, pl.BlockSpec((None, 1, w), lambda i: (layer, 0, 0))],
        out_specs=row,
        out_shape=jax.ShapeDtypeStruct((m, w), BF16),
        compiler_params=_params("parallel"),
        name="gn_gate",
    )(o_f, o_b, grs, gn_g)


def _merge_kernel(ya_ref, yb_ref, yc_ref, w_ref, ga_ref, gb_ref, gc_ref, o_ref, wb_ref):
    @pl.when(pl.program_id(1) == 0)
    def _():
        wb_ref[...] = w_ref[...].astype(BF16)

    out = None
    for i, (y_ref, g_ref) in enumerate(((ya_ref, ga_ref), (yb_ref, gb_ref), (yc_ref, gc_ref))):
        t = g_ref[...].astype(F32) * jnp.dot(y_ref[...], wb_ref[i], preferred_element_type=F32)
        out = t if out is None else out + t
    o_ref[...] = out.astype(o_ref.dtype)


def _merge(y_a, y_b, y_c, gates, w_branch, layer, m):
    d = w_branch.shape[-1]
    bw = w_branch.shape[-2]
    tm = _tile(m, 512, 8)
    tn = _tile(d, 512, 128)
    nj = d // tn
    y_spec = pl.BlockSpec((tm, bw), lambda j, i: (i, 0))
    g_spec = lambda br: pl.BlockSpec((tm, tn), lambda j, i: (i, br * nj + j))
    return pl.pallas_call(
        _merge_kernel,
        grid=(nj, m // tm),
        in_specs=[y_spec, y_spec, y_spec,
                  pl.BlockSpec((None, N_BRANCH, bw, tn), lambda j, i: (layer, 0, 0, j)),
                  g_spec(0), g_spec(1), g_spec(2)],
        out_specs=pl.BlockSpec((tm, tn), lambda j, i: (i, j)),
        out_shape=jax.ShapeDtypeStruct((m, d), BF16),
        scratch_shapes=[pltpu.VMEM((N_BRANCH, bw, tn), BF16)],
        compiler_params=_params("parallel", "arbitrary"),
        name="merge",
    )(y_a, y_b, y_c, w_branch, gates, gates, gates)


def _ln_kernel(lat_ref, y_ref, gate_ref, lng_ref, lnb_ref, *rest, alpha, emit_h):
    z = alpha * lat_ref[...] + gate_ref[...] * y_ref[...]
    mu = jnp.mean(z, axis=-1, keepdims=True)
    var = jnp.mean(jnp.square(z - mu), axis=-1, keepdims=True)
    out = (z - mu) * lax.rsqrt(var + LN_EPS) * lng_ref[...] + lnb_ref[...]
    if emit_h:
        sc_ref, sh_ref, lat_out, h_out = rest
        h_out[...] = (out * (1.0 + sc_ref[...]) + sh_ref[...]).astype(h_out.dtype)
    else:
        lat_out, = rest
    lat_out[...] = out


def _post_ln(lat, y, mod, k_gate, mod_next, k_scale, k_shift, ln_g, ln_b, layer, m, group_of, alpha):
    d = lat.shape[1]
    tm = 256
    emit_h = mod_next is not None
    row = pl.BlockSpec((tm, d), lambda i: (i, 0))
    vec = lambda k: pl.BlockSpec((None, None, 1, d), lambda i: (k, group_of(i * tm), 0, 0))
    par = pl.BlockSpec((None, 1, d), lambda i: (layer, 0, 0))
    in_specs = [row, row, vec(k_gate), par, par]
    args = [lat, y, mod, ln_g, ln_b]
    out_specs = [row]
    out_shape = [jax.ShapeDtypeStruct((m, d), F32)]
    if emit_h:
        in_specs += [vec(k_scale), vec(k_shift)]
        args += [mod_next, mod_next]
        out_specs.append(row)
        out_shape.append(jax.ShapeDtypeStruct((m, d), BF16))
    res = pl.pallas_call(
        functools.partial(_ln_kernel, alpha=alpha, emit_h=emit_h),
        grid=(m // tm,),
        in_specs=in_specs,
        out_specs=out_specs,
        out_shape=out_shape,
        compiler_params=_params("parallel"),
        name="post_ln",
    )(*args)
    return (res[0], res[1]) if emit_h else (res[0], None)


def _conv_kernel(g_ref, prev_ref, next_ref, up_ref, w_ref, b_ref, o_ref, *, tm, lat_rows, seq, ctx_len):
    start = pl.program_id(0) * tm
    in_lat = start < lat_rows
    pos = jnp.where(in_lat, start % seq, (start - lat_rows) % ctx_len)
    length = jnp.where(in_lat, seq, ctx_len)
    first = pos == 0
    last = pos + tm == length
    g = g_ref[...]
    rows = lax.broadcasted_iota(jnp.int32, g.shape, 0)
    prev_row = jnp.where(first, 0.0, prev_ref[7:8, :])
    next_row = jnp.where(last, 0.0, next_ref[0:1, :])
    g_m1 = jnp.where(rows == 0, prev_row, pltpu.roll(g, 1, 0))
    g_p1 = jnp.where(rows == tm - 1, next_row, pltpu.roll(g, tm - 1, 0))
    conv = b_ref[...] + g_m1 * w_ref[0:1, :] + g * w_ref[1:2, :] + g_p1 * w_ref[2:3, :]
    o_ref[...] = (_silu(conv) * up_ref[...]).astype(o_ref.dtype)


def _conv_gate(z, conv_w, conv_b, layer, m, *, lat_rows, seq, ctx_len):
    f = z.shape[1] // 2
    tm = _tile(ctx_len, 128, 8)
    tn = _tile(f, 5632, 128)
    nj = f // tn
    n8 = z.shape[0] // 8
    r8 = tm // 8
    body = functools.partial(_conv_kernel, tm=tm, lat_rows=lat_rows, seq=seq, ctx_len=ctx_len)
    return pl.pallas_call(
        body,
        grid=(m // tm, nj),
        in_specs=[pl.BlockSpec((tm, tn), lambda i, j: (i, j)),
                  pl.BlockSpec((8, tn), lambda i, j: (jnp.maximum(i * r8 - 1, 0), j)),
                  pl.BlockSpec((8, tn), lambda i, j: (jnp.minimum((i + 1) * r8, n8 - 1), j)),
                  pl.BlockSpec((tm, tn), lambda i, j: (i, nj + j)),
                  pl.BlockSpec((None, CONV_WIDTH, tn), lambda i, j: (layer, 0, j)),
                  pl.BlockSpec((None, 1, tn), lambda i, j: (layer, 0, j))],
        out_specs=pl.BlockSpec((tm, tn), lambda i, j: (i, j)),
        out_shape=jax.ShapeDtypeStruct((m, f), BF16),
        compiler_params=_params("parallel", "parallel"),
        name="conv_gate",
    )(z, z, z, z, conv_w, conv_b)


def _rope_tables(seq, n_batch, n_ctx_rows):
    rows = seq // GRID_W
    r_idx, c_idx = jnp.meshgrid(jnp.arange(rows, dtype=F32), jnp.arange(GRID_W, dtype=F32), indexing='ij')
    axis_dim = HEAD_DIM // 2
    inv_freq = ROPE_BASE ** (-jnp.arange(0, axis_dim, 2, dtype=F32) / axis_dim)
    ang_r = r_idx.reshape(-1)[:, None] * inv_freq
    ang_c = c_idx.reshape(-1)[:, None] * inv_freq
    cr, sr, cc, sn = jnp.cos(ang_r), jnp.sin(ang_r), jnp.cos(ang_c), jnp.sin(ang_c)
    zero = jnp.zeros_like(sr)
    c = jnp.concatenate([cr, cr, cc, cc], axis=-1)
    sa = jnp.concatenate([-sr, zero, -sn, zero], axis=-1)
    sb = jnp.concatenate([zero, sr, zero, sn], axis=-1)

    def full(t, fill):
        return jnp.concatenate([jnp.tile(t, (n_batch, 1)), jnp.full((n_ctx_rows, HEAD_DIM), fill, F32)], axis=0)

    return full(c, 1.0), full(sa, 0.0), full(sb, 0.0)


def kernel(x, c, ctx, c_ctx, w_mod, b_mod, w_in, b_gate, w_branch, w_o, attn_sink, q_norm_g, k_norm_g,
           ret_decay, ret_gn_g, ln1_g, ln1_b, w_up, ffn_conv_w, ffn_conv_b, w_down, ln2_g, ln2_b):
    n_batch, seq, d = x.shape
    ctx_len = ctx.shape[1]
    depth = w_mod.shape[0]
    d_ff = w_down.shape[1]
    lat_rows = n_batch * seq
    m_all = lat_rows + n_batch * ctx_len
    alpha = (2 * depth) ** 0.25
    qkv_w = _N_HEADS_QKV * HEAD_DIM
    assert seq % 256 == 0 and ctx_len % 256 == 0 and lat_rows % ctx_len == 0

    def group_of(row_start):
        return jnp.minimum(row_start // seq, n_batch)

    c_rows = jnp.concatenate([c, c_ctx[None], jnp.zeros((8 - n_batch - 1, d), F32)], axis=0)
    mod = _modulation(c_rows, w_mod, b_mod)
    mod = mod.reshape(depth, 8, 6, d).transpose(0, 2, 1, 3)[:, :, :, None, :]
    SH1, SC1, G1, SH2, SC2, G2 = range(6)

    rope_c, rope_sa, rope_sb = _rope_tables(seq, n_batch, n_batch * ctx_len)
    log_gamma = -jnp.exp(ret_decay.astype(F32))
    b_gate3 = b_gate.reshape(depth, 1, -1)
    gn_g3 = ret_gn_g.reshape(depth, 1, -1)
    ln1_g3, ln1_b3 = ln1_g.reshape(depth, 1, d), ln1_b.reshape(depth, 1, d)
    ln2_g3, ln2_b3 = ln2_g.reshape(depth, 1, d), ln2_b.reshape(depth, 1, d)
    conv_b3 = ffn_conv_b.reshape(depth, 1, d_ff)
    w_down_bf = w_down.astype(BF16)

    lat = jnp.concatenate([x.reshape(lat_rows, d), ctx.reshape(n_batch * ctx_len, d)], axis=0)
    h = _modulate(lat, mod[0], SC1, SH1, group_of)

    for l in range(depth):
        last = l == depth - 1
        m = lat_rows if last else m_all

        p = _matmul(h, w_in, l, 0, qkv_w, m_all, F32, name="in_proj_qkv")
        gates = _matmul(h, w_in, l, qkv_w, N_BRANCH * d, m, BF16, bias=b_gate3, name="in_proj_gate")
        act, krf, grs = _prep(p, rope_c, rope_sa, rope_sb, q_norm_g[l][None], k_norm_g[l][None])

        tq = 256
        r128 = tq // 128
        lat128 = seq // 128
        ctx_seg = lambda k0, v0: (ctx_len, lambda b, n: lat_rows // ctx_len + b, k0, v0, None)
        segs_a = [
            (128, lambda b, n: b * lat128 + jnp.maximum(n * r128 - 1, 0), _KA, _VA, -128),
            (tq, lambda b, n: b * (seq // tq) + n, _KA, _VA, 0),
            (128, lambda b, n: b * lat128 + jnp.minimum((n + 1) * r128, lat128 - 1), _KA, _VA, tq),
            ctx_seg(_KA, _VA),
        ]
        attn = functools.partial(_attention, act, n_batch=n_batch)
        y_a = attn(segs_a, q_rows0=0, q_len=seq, q_head0=_QA, n_kv=A_KV_HEADS,
                   group=A_HEADS // A_KV_HEADS, tq=tq, sink=attn_sink[l], name="attn_window")
        segs_b = [ctx_seg(_KB, _VB), (seq, lambda b, n: b, _KB, _VB, None)]
        y_b = attn(segs_b, q_rows0=0, q_len=seq, q_head0=_QB, n_kv=B_KV_HEADS,
                   group=B_HEADS // B_KV_HEADS, tq=128, sink=None, name="attn_dense")
        if not last:
            y_a_c = attn([ctx_seg(_KA, _VA)], q_rows0=lat_rows, q_len=ctx_len, q_head0=_QA,
                         n_kv=A_KV_HEADS, group=A_HEADS // A_KV_HEADS, tq=ctx_len, sink=attn_sink[l],
                         name="attn_ctx_a")
            y_b_c = attn([ctx_seg(_KB, _VB)], q_rows0=lat_rows, q_len=ctx_len, q_head0=_QB,
                         n_kv=B_KV_HEADS, group=B_HEADS // B_KV_HEADS, tq=ctx_len, sink=None,
                         name="attn_ctx_b")
            y_a = jnp.concatenate([y_a, y_a_c], axis=0)
            y_b = jnp.concatenate([y_b, y_b_c], axis=0)

        o_f, o_b = _retention(act, krf, log_gamma[l], n_batch=n_batch, seq=seq, ctx_len=ctx_len)
        y_c = _gn_gate(o_f, o_b, grs, gn_g3, l, m)

        merged = _merge(y_a, y_b, y_c, gates, w_branch, l, m)
        y = _matmul(merged, w_o, l, 0, d, m, F32, name="out_proj")
        lat, h2 = _post_ln(lat, y, mod[l], G1, mod[l], SC2, SH2, ln1_g3, ln1_b3, l, m, group_of, alpha)

        z = _matmul(h2, w_up, l, 0, 2 * d_ff, m, F32, name="up_proj")
        a = _conv_gate(z, ffn_conv_w, conv_b3, l, m, lat_rows=lat_rows, seq=seq, ctx_len=ctx_len)
        y2 = _matmul_a_resident(a, w_down_bf, l, m, F32, name="down_proj")
        nxt = None if last else mod[l + 1]
        lat, h = _post_ln(lat, y2, mod[l], G2, nxt, SC1, SH1, ln2_g3, ln2_b3, l, m, group_of, alpha)

    return lat.reshape(n_batch, seq, d)
```

```python
import functools
import math

import jax
import jax.numpy as jnp
from jax import lax
from jax.experimental import pallas as pl
from jax.experimental.pallas import tpu as pltpu

F32 = jnp.float32
BF16 = jnp.bfloat16

GRID_W = 64
HEAD_DIM = 128
A_HEADS = 8
A_KV_HEADS = 2
B_HEADS = 8
B_KV_HEADS = 2
C_HEADS = 8
WINDOW = 128
RET_CHUNK = 128
N_BRANCH = 3
CONV_WIDTH = 3
ROPE_BASE = 10000.0
LN_EPS = 1e-5
RMS_EPS = 1e-6
GN_EPS = 1e-5
NEG_INF = -1e30

V7X_VMEM_BYTES = 64 * 1024 * 1024
VMEM_LIMIT = V7X_VMEM_BYTES - 8 * 1024 * 1024

_QA, _KA, _VA = 0, 8, 10
_QB, _KB, _VB = 12, 20, 22
_QR, _KR, _VR, _GR = 24, 32, 40, 48
_N_HEADS_QKV = 56
_N_HEADS_ACT = 48


def _params(*sem):
    return pltpu.CompilerParams(dimension_semantics=sem, vmem_limit_bytes=VMEM_LIMIT)


def _tile(n, pref, unit):
    t = min(pref, n) // unit * unit
    while t > unit and n % t:
        t -= unit
    assert t >= unit and n % t == 0, (n, pref, unit)
    return t


def _sigmoid(x):
    return 1.0 / (1.0 + jnp.exp(-x))


def _silu(x):
    return x * _sigmoid(x)


def _mod_kernel(c_ref, w_ref, b_ref, o_ref):
    a = _silu(c_ref[...]).astype(BF16)
    o_ref[...] = jnp.dot(a, w_ref[...].astype(BF16), preferred_element_type=F32) + b_ref[...]


def _modulation(c_rows, w_mod, b_mod):
    depth, d, n = w_mod.shape
    rows = c_rows.shape[0]
    tn = _tile(n, 1024, 128)
    return pl.pallas_call(
        _mod_kernel,
        grid=(depth, n // tn),
        in_specs=[pl.BlockSpec((rows, d), lambda l, j: (0, 0)),
                  pl.BlockSpec((None, d, tn), lambda l, j: (l, 0, j)),
                  pl.BlockSpec((None, 1, tn), lambda l, j: (l, 0, j))],
        out_specs=pl.BlockSpec((None, rows, tn), lambda l, j: (l, 0, j)),
        out_shape=jax.ShapeDtypeStruct((depth, rows, n), F32),
        compiler_params=_params("parallel", "parallel"),
        name="modulation",
    )(c_rows, w_mod, b_mod.reshape(depth, 1, n))


def _modulate_kernel(x_ref, sc_ref, sh_ref, o_ref):
    o_ref[...] = (x_ref[...] * (1.0 + sc_ref[...]) + sh_ref[...]).astype(o_ref.dtype)


def _modulate(x, mod, k_scale, k_shift, group_of):
    m, d = x.shape
    tm = 256
    vec = lambda k: pl.BlockSpec((None, None, 1, d), lambda i: (k, group_of(i * tm), 0, 0))
    return pl.pallas_call(
        _modulate_kernel,
        grid=(m // tm,),
        in_specs=[pl.BlockSpec((tm, d), lambda i: (i, 0)), vec(k_scale), vec(k_shift)],
        out_specs=pl.BlockSpec((tm, d), lambda i: (i, 0)),
        out_shape=jax.ShapeDtypeStruct((m, d), BF16),
        compiler_params=_params("parallel"),
        name="modulate",
    )(x, mod, mod)


def _mm_kernel(a_ref, w_ref, *rest, kc, n_col_tiles, has_bias):
    b_ref = rest[0] if has_bias else None
    o_ref, wb_ref = rest[-2], rest[-1]
    j, i = pl.program_id(0), pl.program_id(1)

    @pl.when(j < n_col_tiles)
    def _():
        wb_ref[j % 2, pl.ds(pl.multiple_of(i * kc, kc), kc), :] = w_ref[...].astype(BF16)

    @pl.when(j > 0)
    def _():
        acc = jnp.dot(a_ref[...], wb_ref[(j + 1) % 2], preferred_element_type=F32)
        if has_bias:
            acc = _sigmoid(acc + b_ref[...])
        o_ref[...] = acc.astype(o_ref.dtype)


def _mm_plain_kernel(a_ref, w_ref, o_ref):
    o_ref[...] = jnp.dot(a_ref[...], w_ref[...], preferred_element_type=F32).astype(o_ref.dtype)


def _matmul_a_resident(a, w, m, out_dtype, *, tm=512, tn=512, name="matmul"):
    k = a.shape[1]
    n = w.shape[-1]
    tm = _tile(m, tm, 8)
    tn = _tile(n, tn, 128)
    return pl.pallas_call(
        _mm_plain_kernel,
        grid=(m // tm, n // tn),
        in_specs=[pl.BlockSpec((tm, k), lambda i, j: (i, 0)),
                  pl.BlockSpec((k, tn), lambda i, j: (0, j))],
        out_specs=pl.BlockSpec((tm, tn), lambda i, j: (i, j)),
        out_shape=jax.ShapeDtypeStruct((m, n), out_dtype),
        compiler_params=_params("parallel", "arbitrary"),
        name=name,
    )(a, w)


_MM_ROW_TILES = 8


def _matmul(a, w, layer, n0, n, m, out_dtype, *, tn=1024, bias=None, name="matmul"):
    k = a.shape[1]
    ni = _MM_ROW_TILES
    tm, kc = m // ni, k // ni
    assert m % ni == 0 and tm % 8 == 0 and k % ni == 0 and kc % 16 == 0
    tn = _tile(math.gcd(n, n0) if n0 else n, tn, 128)
    j0, nj = n0 // tn, n // tn
    row = lambda j, i: jnp.where(j == 0, 0, i)
    col = lambda j: jnp.maximum(j - 1, 0)
    in_specs = [pl.BlockSpec((tm, k), lambda j, i: (row(j, i), 0)),
                pl.BlockSpec((None, kc, tn),
                             lambda j, i: (layer, jnp.where(j == nj, ni - 1, i), jnp.minimum(j, nj - 1) + j0))]
    args = [a, w]
    if bias is not None:
        in_specs.append(pl.BlockSpec((None, 1, tn), lambda j, i: (layer, 0, col(j))))
        args.append(bias)
    return pl.pallas_call(
        functools.partial(_mm_kernel, kc=kc, n_col_tiles=nj, has_bias=bias is not None),
        grid=(nj + 1, ni),
        in_specs=in_specs,
        out_specs=pl.BlockSpec((tm, tn), lambda j, i: (row(j, i), col(j))),
        out_shape=jax.ShapeDtypeStruct((m, n), out_dtype),
        scratch_shapes=[pltpu.VMEM((2, k, tn), BF16)],
        compiler_params=_params("arbitrary", "arbitrary"),
        name=name,
    )(*args)


def _rope(x, c, sa, sb):
    return x * c + pltpu.roll(x, 96, 1) * sa + pltpu.roll(x, 32, 1) * sb


def _rms(x, g):
    return x * lax.rsqrt(jnp.mean(x * x, axis=-1, keepdims=True) + RMS_EPS) * g


def _prep_kernel(p_ref, c_ref, sa_ref, sb_ref, qg_ref, kg_ref, act_ref, krf_ref, grs_ref):
    c, sa, sb = c_ref[...], sa_ref[...], sb_ref[...]
    qg, kg = qg_ref[...], kg_ref[...]
    k_scale = HEAD_DIM ** -0.5
    for h in range(_N_HEADS_QKV):
        cols = slice(h * HEAD_DIM, (h + 1) * HEAD_DIM)
        x = p_ref[:, cols]
        if h < _KA + A_KV_HEADS:
            y = _rope(x, c, sa, sb)
        elif h < _QB:
            y = x
        elif h < _KB:
            y = _rope(_rms(x, qg), c, sa, sb)
        elif h < _VB:
            y = _rope(_rms(x, kg), c, sa, sb)
        elif h < _QR:
            y = x
        elif h < _KR:
            y = _rope(x, c, sa, sb)
        elif h < _VR:
            y = _rope(x, c, sa, sb) * k_scale
            krf_ref[:, (h - _KR) * HEAD_DIM:(h - _KR + 1) * HEAD_DIM] = y
        elif h < _GR:
            y = x
        else:
            grs_ref[:, (h - _GR) * HEAD_DIM:(h - _GR + 1) * HEAD_DIM] = _silu(x)
            continue
        act_ref[:, cols] = y.astype(BF16)


def _prep(p, rope_c, rope_sa, rope_sb, qg, kg):
    m = p.shape[0]
    tm = 256
    row = lambda w: pl.BlockSpec((tm, w), lambda i: (i, 0))
    vec = pl.BlockSpec((1, HEAD_DIM), lambda i: (0, 0))
    return pl.pallas_call(
        _prep_kernel,
        grid=(m // tm,),
        in_specs=[row(_N_HEADS_QKV * HEAD_DIM), row(HEAD_DIM), row(HEAD_DIM), row(HEAD_DIM), vec, vec],
        out_specs=[row(_N_HEADS_ACT * HEAD_DIM), row(C_HEADS * HEAD_DIM), row(C_HEADS * HEAD_DIM)],
        out_shape=[jax.ShapeDtypeStruct((m, _N_HEADS_ACT * HEAD_DIM), BF16),
                   jax.ShapeDtypeStruct((m, C_HEADS * HEAD_DIM), F32),
                   jax.ShapeDtypeStruct((m, C_HEADS * HEAD_DIM), F32)],
        compiler_params=_params("parallel"),
        name="prep",
    )(p, rope_c, rope_sa, rope_sb, qg, kg)


def _attn_head(q_ref, kv, sink_vals, o_ref, o_col0, *, n, bands, tq, seq, group, kchunk):
    n_seg = len(bands)
    scale = HEAD_DIM ** -0.5
    log2e = math.log2(math.e)
    q = jnp.concatenate([q_ref[:, g * HEAD_DIM:(g + 1) * HEAD_DIM] for g in range(group)], axis=0)

    def raw_scores(s, c0, c1):
        sc = lax.dot_general(q, kv[2 * s][c0:c1, :], (((1,), (1,)), ((), ())), preferred_element_type=F32)
        if bands[s] is not None:
            qpos = n * tq + lax.broadcasted_iota(jnp.int32, (tq, c1 - c0), 0)
            kpos = n * tq + (bands[s] + c0) + lax.broadcasted_iota(jnp.int32, (tq, c1 - c0), 1)
            valid = (jnp.abs(qpos - kpos) <= WINDOW) & (kpos >= 0) & (kpos < seq)
            sc = jnp.where(jnp.concatenate([valid] * group, axis=0), sc, NEG_INF)
        return sc

    chunks = [(s, c0, min(c0 + kchunk, kv[2 * s].shape[0]))
              for s in range(n_seg) for c0 in range(0, kv[2 * s].shape[0], kchunk)]
    scores = [raw_scores(*ch) for ch in chunks]
    mx = functools.reduce(jnp.maximum, [jnp.max(sc, axis=-1, keepdims=True) for sc in scores])
    mx = mx * scale
    if sink_vals is not None:
        sink = jnp.concatenate([jnp.full((tq, 1), v, F32) for v in sink_vals], axis=0)
        mx = jnp.maximum(mx, sink)
    mx2 = mx * log2e
    den, out = None, None
    for (s, c0, c1), sc in zip(chunks, scores):
        e = jnp.exp2(sc * (scale * log2e) - mx2)
        r = jnp.sum(e, axis=-1, keepdims=True)
        o = jnp.dot(e.astype(BF16), kv[2 * s + 1][c0:c1, :], preferred_element_type=F32)
        den = r if den is None else den + r
        out = o if out is None else out + o
    if sink_vals is not None:
        den = den + jnp.exp2(sink * log2e - mx2)
    out = out / den
    for g in range(group):
        c = o_col0 + g * HEAD_DIM
        o_ref[:, c:c + HEAD_DIM] = out[g * tq:(g + 1) * tq].astype(o_ref.dtype)


def _attn_kernel(*refs, n_kv, n_seg, bands, has_sink, tq, seq, group, kchunk):
    sink_ref = refs[-2] if has_sink else None
    o_ref = refs[-1]
    n = pl.program_id(1)
    for h in range(n_kv):
        kv = refs[n_kv + h * 2 * n_seg:n_kv + (h + 1) * 2 * n_seg]
        sink_vals = [sink_ref[h * group + g] for g in range(group)] if has_sink else None
        _attn_head(refs[h], kv, sink_vals, o_ref, h * group * HEAD_DIM,
                   n=n, bands=bands, tq=tq, seq=seq, group=group, kchunk=kchunk)


def _attention(act, segs, *, n_batch, q_rows0, q_len, q_head0, n_kv, group, tq, sink, name):
    gw = group * HEAD_DIM
    nq = q_len // tq
    assert q_rows0 % tq == 0 and q_len % tq == 0 and q_head0 % group == 0
    in_specs, args = [], []
    for h in range(n_kv):
        in_specs.append(pl.BlockSpec(
            (tq, gw), lambda b, n, h=h: (q_rows0 // tq + b * nq + n, q_head0 // group + h)))
        args.append(act)
    for h in range(n_kv):
        for rows, row_fn, k0, v0, _ in segs:
            for c0 in (k0, v0):
                in_specs.append(pl.BlockSpec(
                    (rows, HEAD_DIM), lambda b, n, row_fn=row_fn, c=c0 + h: (row_fn(b, n), c)))
                args.append(act)
    if sink is not None:
        in_specs.append(pl.BlockSpec(memory_space=pltpu.SMEM))
        args.append(sink)
    body = functools.partial(
        _attn_kernel, n_kv=n_kv, n_seg=len(segs), bands=tuple(s[4] for s in segs),
        has_sink=sink is not None, tq=tq, seq=q_len, group=group, kchunk=512)
    return pl.pallas_call(
        body,
        grid=(n_batch, nq),
        in_specs=in_specs,
        out_specs=pl.BlockSpec((tq, n_kv * gw), lambda b, n: (b * nq + n, 0)),
        out_shape=jax.ShapeDtypeStruct((n_batch * q_len, n_kv * gw), BF16),
        compiler_params=_params("parallel", "arbitrary"),
        name=name,
    )(*args)


def _retention_kernel(lg_ref, qf_ref, kf_ref, kff_ref, vf_ref, qb_ref, kb_ref, kbf_ref, vb_ref,
                      of_ref, ob_ref, state_ref):
    c = RET_CHUNK

    @pl.when(pl.program_id(1) == 0)
    def _():
        state_ref[...] = jnp.zeros_like(state_ref)

    row = lax.broadcasted_iota(jnp.int32, (c, c), 0)
    col = lax.broadcasted_iota(jnp.int32, (c, c), 1)
    pos = lax.broadcasted_iota(jnp.int32, (c, 1), 0).astype(F32)
    dirs = ((qf_ref, kf_ref, kff_ref, vf_ref, of_ref), (qb_ref, kb_ref, kbf_ref, vb_ref, ob_ref))
    for d, (q_ref, k_ref, kfull_ref, v_ref, o_ref) in enumerate(dirs):
        diff = (row - col) if d == 0 else (col - row)
        keep = (diff >= 0) if d == 0 else (diff > 0)
        dist = jnp.maximum(diff, 0).astype(F32)
        q_pow = (pos + 1.0) if d == 0 else (c - pos)
        k_pow = (c - 1.0 - pos) if d == 0 else pos
        for h in range(C_HEADS):
            cols = slice(h * HEAD_DIM, (h + 1) * HEAD_DIM)
            lg = lg_ref[d, h]
            d_intra = jnp.where(keep, jnp.exp(lg * dist), 0.0)
            d_q = jnp.exp(lg * q_pow)
            d_k = jnp.exp(lg * k_pow)
            d_c = jnp.exp(jnp.full((1, 1), lg * c, F32))
            q, k, v = q_ref[:, cols], k_ref[:, cols], v_ref[:, cols]
            state = state_ref[d, h]
            att = lax.dot_general(q, k, (((1,), (1,)), ((), ())), preferred_element_type=F32) * d_intra
            o = (jnp.dot(att.astype(BF16), v, preferred_element_type=F32)
                 + jnp.dot(q, state.astype(BF16), preferred_element_type=F32) * d_q)
            o_ref[:, cols] = o
            kd = (kfull_ref[:, cols] * d_k).T.astype(BF16)
            state_ref[d, h] = state * d_c + jnp.dot(kd, v, preferred_element_type=F32)


def _retention(act, krf, log_gamma, *, n_batch, seq, ctx_len):
    c = RET_CHUNK
    w = C_HEADS * HEAD_DIM
    m = act.shape[0]
    n_ctx, n_lat = ctx_len // c, seq // c
    ctx0 = n_batch * seq // c

    def fwd(b, t):
        return jnp.where(t < n_ctx, ctx0 + b * n_ctx + t, b * n_lat + t - n_ctx)

    def bwd(b, t):
        return jnp.where(t < n_ctx, ctx0 + b * n_ctx + (n_ctx - 1 - t), b * n_lat + (n_lat - 1 - (t - n_ctx)))

    def spec(row_fn, col_block):
        return pl.BlockSpec((c, w), lambda b, t: (row_fn(b, t), col_block))

    qc, kc, vc = _QR * HEAD_DIM // w, _KR * HEAD_DIM // w, _VR * HEAD_DIM // w
    return pl.pallas_call(
        _retention_kernel,
        grid=(n_batch, n_ctx + n_lat),
        in_specs=[pl.BlockSpec(memory_space=pltpu.SMEM),
                  spec(fwd, qc), spec(fwd, kc), spec(fwd, 0), spec(fwd, vc),
                  spec(bwd, qc), spec(bwd, kc), spec(bwd, 0), spec(bwd, vc)],
        out_specs=[spec(fwd, 0), spec(bwd, 0)],
        out_shape=[jax.ShapeDtypeStruct((m, w), F32), jax.ShapeDtypeStruct((m, w), F32)],
        scratch_shapes=[pltpu.VMEM((2, C_HEADS, HEAD_DIM, HEAD_DIM), F32)],
        compiler_params=_params("parallel", "arbitrary"),
        name="retention",
    )(log_gamma, act, act, krf, act, act, act, krf, act)


def _gn_gate_kernel(of_ref, ob_ref, grs_ref, g_ref, o_ref):
    for h in range(C_HEADS):
        cols = slice(h * HEAD_DIM, (h + 1) * HEAD_DIM)
        o = of_ref[:, cols] + ob_ref[:, cols]
        mu = jnp.mean(o, axis=-1, keepdims=True)
        var = jnp.mean(jnp.square(o - mu), axis=-1, keepdims=True)
        y = (o - mu) * lax.rsqrt(var + GN_EPS) * g_ref[:, cols]
        o_ref[:, cols] = (grs_ref[:, cols] * y).astype(o_ref.dtype)


def _gn_gate(o_f, o_b, grs, gn_g, layer, m):
    w = o_f.shape[1]
    tm = 256
    row = pl.BlockSpec((tm, w), lambda i: (i, 0))
    return pl.pallas_call(
        _gn_gate_kernel,
        grid=(m // tm,),
        in_specs=[row, row, row, pl.BlockSpec((None, 1, w), lambda i: (layer, 0, 0))],
        out_specs=row,
        out_shape=jax.ShapeDtypeStruct((m, w), BF16),
        compiler_params=_params("parallel"),
        name="gn_gate",
    )(o_f, o_b, grs, gn_g)


def _merge_kernel(ya_ref, yb_ref, yc_ref, w_ref, ga_ref, gb_ref, gc_ref, o_ref, wb_ref):
    @pl.when(pl.program_id(1) == 0)
    def _():
        wb_ref[...] = w_ref[...].astype(BF16)

    out = None
    for i, (y_ref, g_ref) in enumerate(((ya_ref, ga_ref), (yb_ref, gb_ref), (yc_ref, gc_ref))):
        t = g_ref[...].astype(F32) * jnp.dot(y_ref[...], wb_ref[i], preferred_element_type=F32)
        out = t if out is None else out + t
    o_ref[...] = out.astype(o_ref.dtype)


def _merge(y_a, y_b, y_c, gates, w_branch, layer, m):
    d = w_branch.shape[-1]
    bw = w_branch.shape[-2]
    tm = _tile(m, 512, 8)
    tn = _tile(d, 512, 128)
    nj = d // tn
    y_spec = pl.BlockSpec((tm, bw), lambda j, i: (i, 0))
    g_spec = lambda br: pl.BlockSpec((tm, tn), lambda j, i: (i, br * nj + j))
    return pl.pallas_call(
        _merge_kernel,
        grid=(nj, m // tm),
        in_specs=[y_spec, y_spec, y_spec,
                  pl.BlockSpec((None, N_BRANCH, bw, tn), lambda j, i: (layer, 0, 0, j)),
                  g_spec(0), g_spec(1), g_spec(2)],
        out_specs=pl.BlockSpec((tm, tn), lambda j, i: (i, j)),
        out_shape=jax.ShapeDtypeStruct((m, d), BF16),
        scratch_shapes=[pltpu.VMEM((N_BRANCH, bw, tn), BF16)],
        compiler_params=_params("parallel", "arbitrary"),
        name="merge",
    )(y_a, y_b, y_c, w_branch, gates, gates, gates)


def _ln_kernel(lat_ref, y_ref, gate_ref, lng_ref, lnb_ref, *rest, alpha, emit_h):
    z = alpha * lat_ref[...] + gate_ref[...] * y_ref[...]
    mu = jnp.mean(z, axis=-1, keepdims=True)
    var = jnp.mean(jnp.square(z - mu), axis=-1, keepdims=True)
    out = (z - mu) * lax.rsqrt(var + LN_EPS) * lng_ref[...] + lnb_ref[...]
    if emit_h:
        sc_ref, sh_ref, lat_out, h_out = rest
        h_out[...] = (out * (1.0 + sc_ref[...]) + sh_ref[...]).astype(h_out.dtype)
    else:
        lat_out, = rest
    lat_out[...] = out


def _post_ln(lat, y, mod, k_gate, mod_next, k_scale, k_shift, ln_g, ln_b, layer, m, group_of, alpha):
    d = lat.shape[1]
    tm = 256
    emit_h = mod_next is not None
    row = pl.BlockSpec((tm, d), lambda i: (i, 0))
    vec = lambda k: pl.BlockSpec((None, None, 1, d), lambda i: (k, group_of(i * tm), 0, 0))
    par = pl.BlockSpec((None, 1, d), lambda i: (layer, 0, 0))
    in_specs = [row, row, vec(k_gate), par, par]
    args = [lat, y, mod, ln_g, ln_b]
    out_specs = [row]
    out_shape = [jax.ShapeDtypeStruct((m, d), F32)]
    if emit_h:
        in_specs += [vec(k_scale), vec(k_shift)]
        args += [mod_next, mod_next]
        out_specs.append(row)
        out_shape.append(jax.ShapeDtypeStruct((m, d), BF16))
    res = pl.pallas_call(
        functools.partial(_ln_kernel, alpha=alpha, emit_h=emit_h),
        grid=(m // tm,),
        in_specs=in_specs,
        out_specs=out_specs,
        out_shape=out_shape,
        compiler_params=_params("parallel"),
        name="post_ln",
    )(*args)
    return (res[0], res[1]) if emit_h else (res[0], None)


_HALO = 8


def _up_conv_kernel(a_ref, wg_ref, wu_ref, cw_ref, cb_ref, wd_ref, o_ref, wd_out_ref, wb_ref, *,
                    kc, n_col_tiles, tm, n_sub, seq_first_rows, seq_last_rows):
    j, i = pl.program_id(0), pl.program_id(1)
    wd_out_ref[...] = wd_ref[...].astype(BF16)

    @pl.when(j < n_col_tiles)
    def _():
        rows = pl.ds(pl.multiple_of(i * kc, kc), kc)
        wb_ref[j % 2, 0, rows, :] = wg_ref[...].astype(BF16)
        wb_ref[j % 2, 1, rows, :] = wu_ref[...].astype(BF16)

    @pl.when(j > 0)
    def _():
        slot = (j + 1) % 2
        ts = tm // n_sub
        n_rows = ts + 2 * _HALO
        for k in range(n_sub):
            r0 = k * ts
            g = jnp.dot(a_ref[r0:r0 + n_rows, :], wb_ref[slot, 0], preferred_element_type=F32)
            u = jnp.dot(a_ref[r0 + _HALO:r0 + _HALO + ts, :], wb_ref[slot, 1], preferred_element_type=F32)
            g_prev = pltpu.roll(g, 1, 0)[_HALO:_HALO + ts]
            g_next = pltpu.roll(g, n_rows - 1, 0)[_HALO:_HALO + ts]
            row = i * tm + r0 + lax.broadcasted_iota(jnp.int32, (ts, 1), 0)
            has_prev = functools.reduce(lambda x, y: x & y, [row != r for r in seq_first_rows])
            has_next = functools.reduce(lambda x, y: x & y, [row != r for r in seq_last_rows])
            conv = (cb_ref[...] + jnp.where(has_prev, g_prev, 0.0) * cw_ref[0:1, :]
                    + g[_HALO:_HALO + ts] * cw_ref[1:2, :] + jnp.where(has_next, g_next, 0.0) * cw_ref[2:3, :])
            o_ref[r0:r0 + ts, :] = (_silu(conv) * u).astype(o_ref.dtype)


def _up_conv_gate(h, w_up, conv_w, conv_b, w_down, layer, m, seq_lens):
    k = h.shape[1]
    f = w_up.shape[-1] // 2
    d_out = w_down.shape[-1]
    ni = _MM_ROW_TILES
    tm, kc = m // ni, k // ni
    assert m % ni == 0 and tm % 8 == 0 and k % ni == 0 and kc % 16 == 0 and sum(seq_lens) == m
    tf = _tile(f, 256, 128)
    nj = f // tf
    wd_rows = f // (nj * ni)
    assert f % (nj * ni) == 0 and wd_rows % 16 == 0
    wd_blk = lambda j, i: jnp.maximum(j - 1, 0) * ni + jnp.where(j == 0, 0, i)
    starts = [sum(seq_lens[:s]) for s in range(len(seq_lens))]
    first_rows = tuple(starts)
    last_rows = tuple(st + n - 1 for st, n in zip(starts, seq_lens))
    hp = jnp.pad(h[:m], ((_HALO, _HALO), (0, 0)))
    tiles = jnp.stack([hp[t * tm:t * tm + tm + 2 * _HALO] for t in range(ni)])
    row = lambda j, i: jnp.where(j == 0, 0, i)
    col = lambda j: jnp.maximum(j - 1, 0)
    chunk = lambda j, i: jnp.where(j == nj, ni - 1, i)
    stage = lambda j: jnp.minimum(j, nj - 1)
    n_sub = 2 if tm % 32 == 0 else 1
    body = functools.partial(_up_conv_kernel, kc=kc, n_col_tiles=nj, tm=tm, n_sub=n_sub,
                             seq_first_rows=first_rows, seq_last_rows=last_rows)
    return pl.pallas_call(
        body,
        grid=(nj + 1, ni),
        in_specs=[pl.BlockSpec((None, tm + 2 * _HALO, k), lambda j, i: (row(j, i), 0, 0)),
                  pl.BlockSpec((None, kc, tf), lambda j, i: (layer, chunk(j, i), stage(j))),
                  pl.BlockSpec((None, kc, tf), lambda j, i: (layer, chunk(j, i), nj + stage(j))),
                  pl.BlockSpec((None, CONV_WIDTH, tf), lambda j, i: (layer, 0, col(j))),
                  pl.BlockSpec((None, 1, tf), lambda j, i: (layer, 0, col(j))),
                  pl.BlockSpec((None, wd_rows, d_out), lambda j, i: (layer, wd_blk(j, i), 0))],
        out_specs=[pl.BlockSpec((tm, tf), lambda j, i: (row(j, i), col(j))),
                   pl.BlockSpec((wd_rows, d_out), lambda j, i: (wd_blk(j, i), 0))],
        out_shape=[jax.ShapeDtypeStruct((m, f), BF16), jax.ShapeDtypeStruct((f, d_out), BF16)],
        scratch_shapes=[pltpu.VMEM((2, 2, k, tf), BF16)],
        compiler_params=_params("arbitrary", "arbitrary"),
        name="up_conv_gate",
    )(tiles, w_up, w_up, conv_w, conv_b, w_down)


def _rope_tables(seq, n_batch, n_ctx_rows):
    rows = seq // GRID_W
    r_idx, c_idx = jnp.meshgrid(jnp.arange(rows, dtype=F32), jnp.arange(GRID_W, dtype=F32), indexing='ij')
    axis_dim = HEAD_DIM // 2
    inv_freq = ROPE_BASE ** (-jnp.arange(0, axis_dim, 2, dtype=F32) / axis_dim)
    ang_r = r_idx.reshape(-1)[:, None] * inv_freq
    ang_c = c_idx.reshape(-1)[:, None] * inv_freq
    cr, sr, cc, sn = jnp.cos(ang_r), jnp.sin(ang_r), jnp.cos(ang_c), jnp.sin(ang_c)
    zero = jnp.zeros_like(sr)
    c = jnp.concatenate([cr, cr, cc, cc], axis=-1)
    sa = jnp.concatenate([-sr, zero, -sn, zero], axis=-1)
    sb = jnp.concatenate([zero, sr, zero, sn], axis=-1)

    def full(t, fill):
        return jnp.concatenate([jnp.tile(t, (n_batch, 1)), jnp.full((n_ctx_rows, HEAD_DIM), fill, F32)], axis=0)

    return full(c, 1.0), full(sa, 0.0), full(sb, 0.0)


def kernel(x, c, ctx, c_ctx, w_mod, b_mod, w_in, b_gate, w_branch, w_o, attn_sink, q_norm_g, k_norm_g,
           ret_decay, ret_gn_g, ln1_g, ln1_b, w_up, ffn_conv_w, ffn_conv_b, w_down, ln2_g, ln2_b):
    n_batch, seq, d = x.shape
    ctx_len = ctx.shape[1]
    depth = w_mod.shape[0]
    d_ff = w_down.shape[1]
    lat_rows = n_batch * seq
    m_all = lat_rows + n_batch * ctx_len
    alpha = (2 * depth) ** 0.25
    qkv_w = _N_HEADS_QKV * HEAD_DIM
    assert seq % 256 == 0 and ctx_len % 256 == 0 and lat_rows % ctx_len == 0

    def group_of(row_start):
        return jnp.minimum(row_start // seq, n_batch)

    c_rows = jnp.concatenate([c, c_ctx[None], jnp.zeros((8 - n_batch - 1, d), F32)], axis=0)
    mod = _modulation(c_rows, w_mod, b_mod)
    mod = mod.reshape(depth, 8, 6, d).transpose(0, 2, 1, 3)[:, :, :, None, :]
    SH1, SC1, G1, SH2, SC2, G2 = range(6)

    rope_c, rope_sa, rope_sb = _rope_tables(seq, n_batch, n_batch * ctx_len)
    log_gamma = -jnp.exp(ret_decay.astype(F32))
    b_gate3 = b_gate.reshape(depth, 1, -1)
    gn_g3 = ret_gn_g.reshape(depth, 1, -1)
    ln1_g3, ln1_b3 = ln1_g.reshape(depth, 1, d), ln1_b.reshape(depth, 1, d)
    ln2_g3, ln2_b3 = ln2_g.reshape(depth, 1, d), ln2_b.reshape(depth, 1, d)
    conv_b3 = ffn_conv_b.reshape(depth, 1, d_ff)

    lat = jnp.concatenate([x.reshape(lat_rows, d), ctx.reshape(n_batch * ctx_len, d)], axis=0)
    h = _modulate(lat, mod[0], SC1, SH1, group_of)

    for l in range(depth):
        last = l == depth - 1
        m = lat_rows if last else m_all

        p = _matmul(h, w_in, l, 0, qkv_w, m_all, F32, name="in_proj_qkv")
        gates = _matmul(h, w_in, l, qkv_w, N_BRANCH * d, m, BF16, bias=b_gate3, name="in_proj_gate")
        act, krf, grs = _prep(p, rope_c, rope_sa, rope_sb, q_norm_g[l][None], k_norm_g[l][None])

        tq = 256
        r128 = tq // 128
        lat128 = seq // 128
        ctx_seg = lambda k0, v0: (ctx_len, lambda b, n: lat_rows // ctx_len + b, k0, v0, None)
        segs_a = [
            (128, lambda b, n: b * lat128 + jnp.maximum(n * r128 - 1, 0), _KA, _VA, -128),
            (tq, lambda b, n: b * (seq // tq) + n, _KA, _VA, 0),
            (128, lambda b, n: b * lat128 + jnp.minimum((n + 1) * r128, lat128 - 1), _KA, _VA, tq),
            ctx_seg(_KA, _VA),
        ]
        attn = functools.partial(_attention, act, n_batch=n_batch)
        y_a = attn(segs_a, q_rows0=0, q_len=seq, q_head0=_QA, n_kv=A_KV_HEADS,
                   group=A_HEADS // A_KV_HEADS, tq=tq, sink=attn_sink[l], name="attn_window")
        segs_b = [ctx_seg(_KB, _VB), (seq, lambda b, n: b, _KB, _VB, None)]
        y_b = attn(segs_b, q_rows0=0, q_len=seq, q_head0=_QB, n_kv=B_KV_HEADS,
                   group=B_HEADS // B_KV_HEADS, tq=128, sink=None, name="attn_dense")
        if not last:
            y_a_c = attn([ctx_seg(_KA, _VA)], q_rows0=lat_rows, q_len=ctx_len, q_head0=_QA,
                         n_kv=A_KV_HEADS, group=A_HEADS // A_KV_HEADS, tq=ctx_len, sink=attn_sink[l],
                         name="attn_ctx_a")
            y_b_c = attn([ctx_seg(_KB, _VB)], q_rows0=lat_rows, q_len=ctx_len, q_head0=_QB,
                         n_kv=B_KV_HEADS, group=B_HEADS // B_KV_HEADS, tq=ctx_len, sink=None,
                         name="attn_ctx_b")
            y_a = jnp.concatenate([y_a, y_a_c], axis=0)
            y_b = jnp.concatenate([y_b, y_b_c], axis=0)

        o_f, o_b = _retention(act, krf, log_gamma[l], n_batch=n_batch, seq=seq, ctx_len=ctx_len)
        y_c = _gn_gate(o_f, o_b, grs, gn_g3, l, m)

        merged = _merge(y_a, y_b, y_c, gates, w_branch, l, m)
        y = _matmul(merged, w_o, l, 0, d, m, F32, name="out_proj")
        lat, h2 = _post_ln(lat, y, mod[l], G1, mod[l], SC2, SH2, ln1_g3, ln1_b3, l, m, group_of, alpha)

        seq_lens = [seq] * n_batch + ([] if last else [ctx_len] * n_batch)
        a, w_down_bf = _up_conv_gate(h2, w_up, ffn_conv_w, conv_b3, w_down, l, m, seq_lens)
        y2 = _matmul_a_resident(a, w_down_bf, m, F32, name="down_proj")
        nxt = None if last else mod[l + 1]
        lat, h = _post_ln(lat, y2, mod[l], G2, nxt, SC1, SH1, ln2_g3, ln2_b3, l, m, group_of, alpha)

    return lat.reshape(n_batch, seq, d)
```

```python
import functools
import math

import jax
import jax.numpy as jnp
from jax import lax
from jax.experimental import pallas as pl
from jax.experimental.pallas import tpu as pltpu

F32 = jnp.float32
BF16 = jnp.bfloat16

GRID_W = 64
HEAD_DIM = 128
A_HEADS = 8
A_KV_HEADS = 2
B_HEADS = 8
B_KV_HEADS = 2
C_HEADS = 8
WINDOW = 128
RET_CHUNK = 128
N_BRANCH = 3
CONV_WIDTH = 3
ROPE_BASE = 10000.0
LN_EPS = 1e-5
RMS_EPS = 1e-6
GN_EPS = 1e-5
NEG_INF = -1e30

V7X_VMEM_BYTES = 64 * 1024 * 1024
VMEM_LIMIT = V7X_VMEM_BYTES - 8 * 1024 * 1024

_QA, _KA, _VA = 0, 8, 10
_QB, _KB, _VB = 12, 20, 22
_QR, _KR, _VR, _GR = 24, 32, 40, 48
_N_HEADS_QKV = 56
_N_HEADS_ACT = 48


def _params(*sem):
    return pltpu.CompilerParams(dimension_semantics=sem, vmem_limit_bytes=VMEM_LIMIT)


def _tile(n, pref, unit):
    t = min(pref, n) // unit * unit
    while t > unit and n % t:
        t -= unit
    assert t >= unit and n % t == 0, (n, pref, unit)
    return t


def _sigmoid(x):
    return 1.0 / (1.0 + jnp.exp(-x))


def _silu(x):
    return x * _sigmoid(x)


def _mod_kernel(c_ref, w_ref, b_ref, o_ref):
    a = _silu(c_ref[...]).astype(BF16)
    o_ref[...] = jnp.dot(a, w_ref[...].astype(BF16), preferred_element_type=F32) + b_ref[...]


def _modulation(c_rows, w_mod, b_mod):
    depth, d, n = w_mod.shape
    rows = c_rows.shape[0]
    tn = _tile(n, 1024, 128)
    return pl.pallas_call(
        _mod_kernel,
        grid=(depth, n // tn),
        in_specs=[pl.BlockSpec((rows, d), lambda l, j: (0, 0)),
                  pl.BlockSpec((None, d, tn), lambda l, j: (l, 0, j)),
                  pl.BlockSpec((None, 1, tn), lambda l, j: (l, 0, j))],
        out_specs=pl.BlockSpec((None, rows, tn), lambda l, j: (l, 0, j)),
        out_shape=jax.ShapeDtypeStruct((depth, rows, n), F32),
        compiler_params=_params("parallel", "parallel"),
        name="modulation",
    )(c_rows, w_mod, b_mod.reshape(depth, 1, n))


def _modulate_kernel(x_ref, sc_ref, sh_ref, o_ref):
    o_ref[...] = (x_ref[...] * (1.0 + sc_ref[...]) + sh_ref[...]).astype(o_ref.dtype)


def _modulate(x, mod, k_scale, k_shift, group_of):
    m, d = x.shape
    tm = 256
    vec = lambda k: pl.BlockSpec((None, None, 1, d), lambda i: (k, group_of(i * tm), 0, 0))
    return pl.pallas_call(
        _modulate_kernel,
        grid=(m // tm,),
        in_specs=[pl.BlockSpec((tm, d), lambda i: (i, 0)), vec(k_scale), vec(k_shift)],
        out_specs=pl.BlockSpec((tm, d), lambda i: (i, 0)),
        out_shape=jax.ShapeDtypeStruct((m, d), BF16),
        compiler_params=_params("parallel"),
        name="modulate",
    )(x, mod, mod)


def _mm_kernel(a_ref, w_ref, *rest, kc, n_col_tiles, has_bias):
    b_ref = rest[0] if has_bias else None
    o_ref, wb_ref = rest[-2], rest[-1]
    j, i = pl.program_id(0), pl.program_id(1)

    @pl.when(j < n_col_tiles)
    def _():
        wb_ref[j % 2, pl.ds(pl.multiple_of(i * kc, kc), kc), :] = w_ref[...].astype(BF16)

    @pl.when(j > 0)
    def _():
        acc = jnp.dot(a_ref[...], wb_ref[(j + 1) % 2], preferred_element_type=F32)
        if has_bias:
            acc = _sigmoid(acc + b_ref[...])
        o_ref[...] = acc.astype(o_ref.dtype)


def _mm_plain_kernel(a_ref, w_ref, o_ref):
    o_ref[...] = jnp.dot(a_ref[...], w_ref[...], preferred_element_type=F32).astype(o_ref.dtype)


def _matmul_a_resident(a, w, m, out_dtype, *, tm=512, tn=512, name="matmul"):
    k = a.shape[1]
    n = w.shape[-1]
    tm = _tile(m, tm, 8)
    tn = _tile(n, tn, 128)
    return pl.pallas_call(
        _mm_plain_kernel,
        grid=(m // tm, n // tn),
        in_specs=[pl.BlockSpec((tm, k), lambda i, j: (i, 0)),
                  pl.BlockSpec((k, tn), lambda i, j: (0, j))],
        out_specs=pl.BlockSpec((tm, tn), lambda i, j: (i, j)),
        out_shape=jax.ShapeDtypeStruct((m, n), out_dtype),
        compiler_params=_params("parallel", "arbitrary"),
        name=name,
    )(a, w)


_MM_ROW_TILES = 8


def _matmul(a, w, layer, n0, n, m, out_dtype, *, tn=1024, bias=None, name="matmul"):
    k = a.shape[1]
    ni = _MM_ROW_TILES
    tm, kc = m // ni, k // ni
    assert m % ni == 0 and tm % 8 == 0 and k % ni == 0 and kc % 16 == 0
    tn = _tile(math.gcd(n, n0) if n0 else n, tn, 128)
    j0, nj = n0 // tn, n // tn
    row = lambda j, i: jnp.where(j == 0, 0, i)
    col = lambda j: jnp.maximum(j - 1, 0)
    in_specs = [pl.BlockSpec((tm, k), lambda j, i: (row(j, i), 0)),
                pl.BlockSpec((None, kc, tn),
                             lambda j, i: (layer, jnp.where(j == nj, ni - 1, i), jnp.minimum(j, nj - 1) + j0))]
    args = [a, w]
    if bias is not None:
        in_specs.append(pl.BlockSpec((None, 1, tn), lambda j, i: (layer, 0, col(j))))
        args.append(bias)
    return pl.pallas_call(
        functools.partial(_mm_kernel, kc=kc, n_col_tiles=nj, has_bias=bias is not None),
        grid=(nj + 1, ni),
        in_specs=in_specs,
        out_specs=pl.BlockSpec((tm, tn), lambda j, i: (row(j, i), col(j))),
        out_shape=jax.ShapeDtypeStruct((m, n), out_dtype),
        scratch_shapes=[pltpu.VMEM((2, k, tn), BF16)],
        compiler_params=_params("arbitrary", "arbitrary"),
        name=name,
    )(*args)


def _rope(x, c, sa, sb):
    return x * c + pltpu.roll(x, 96, 1) * sa + pltpu.roll(x, 32, 1) * sb


def _rms(x, g):
    return x * lax.rsqrt(jnp.mean(x * x, axis=-1, keepdims=True) + RMS_EPS) * g


def _prep_kernel(p_ref, c_ref, sa_ref, sb_ref, qg_ref, kg_ref, act_ref, krf_ref, grs_ref):
    c, sa, sb = c_ref[...], sa_ref[...], sb_ref[...]
    qg, kg = qg_ref[...], kg_ref[...]
    k_scale = HEAD_DIM ** -0.5
    for h in range(_N_HEADS_QKV):
        cols = slice(h * HEAD_DIM, (h + 1) * HEAD_DIM)
        x = p_ref[:, cols]
        if h < _KA + A_KV_HEADS:
            y = _rope(x, c, sa, sb)
        elif h < _QB:
            y = x
        elif h < _KB:
            y = _rope(_rms(x, qg), c, sa, sb)
        elif h < _VB:
            y = _rope(_rms(x, kg), c, sa, sb)
        elif h < _QR:
            y = x
        elif h < _KR:
            y = _rope(x, c, sa, sb)
        elif h < _VR:
            y = _rope(x, c, sa, sb) * k_scale
            krf_ref[:, (h - _KR) * HEAD_DIM:(h - _KR + 1) * HEAD_DIM] = y
        elif h < _GR:
            y = x
        else:
            grs_ref[:, (h - _GR) * HEAD_DIM:(h - _GR + 1) * HEAD_DIM] = _silu(x)
            continue
        act_ref[:, cols] = y.astype(BF16)


def _prep(p, rope_c, rope_sa, rope_sb, qg, kg):
    m = p.shape[0]
    tm = 256
    row = lambda w: pl.BlockSpec((tm, w), lambda i: (i, 0))
    vec = pl.BlockSpec((1, HEAD_DIM), lambda i: (0, 0))
    return pl.pallas_call(
        _prep_kernel,
        grid=(m // tm,),
        in_specs=[row(_N_HEADS_QKV * HEAD_DIM), row(HEAD_DIM), row(HEAD_DIM), row(HEAD_DIM), vec, vec],
        out_specs=[row(_N_HEADS_ACT * HEAD_DIM), row(C_HEADS * HEAD_DIM), row(C_HEADS * HEAD_DIM)],
        out_shape=[jax.ShapeDtypeStruct((m, _N_HEADS_ACT * HEAD_DIM), BF16),
                   jax.ShapeDtypeStruct((m, C_HEADS * HEAD_DIM), F32),
                   jax.ShapeDtypeStruct((m, C_HEADS * HEAD_DIM), F32)],
        compiler_params=_params("parallel"),
        name="prep",
    )(p, rope_c, rope_sa, rope_sb, qg, kg)


def _attn_head(q_ref, kv, sink_vals, o_ref, o_col0, *, n, bands, tq, seq, group, kchunk):
    n_seg = len(bands)
    scale = HEAD_DIM ** -0.5
    log2e = math.log2(math.e)
    q = jnp.concatenate([q_ref[:, g * HEAD_DIM:(g + 1) * HEAD_DIM] for g in range(group)], axis=0)

    def raw_scores(s, c0, c1):
        sc = lax.dot_general(q, kv[2 * s][c0:c1, :], (((1,), (1,)), ((), ())), preferred_element_type=F32)
        if bands[s] is not None:
            qpos = n * tq + lax.broadcasted_iota(jnp.int32, (tq, c1 - c0), 0)
            kpos = n * tq + (bands[s] + c0) + lax.broadcasted_iota(jnp.int32, (tq, c1 - c0), 1)
            valid = (jnp.abs(qpos - kpos) <= WINDOW) & (kpos >= 0) & (kpos < seq)
            sc = jnp.where(jnp.concatenate([valid] * group, axis=0), sc, NEG_INF)
        return sc

    chunks = [(s, c0, min(c0 + kchunk, kv[2 * s].shape[0]))
              for s in range(n_seg) for c0 in range(0, kv[2 * s].shape[0], kchunk)]
    scores = [raw_scores(*ch) for ch in chunks]
    mx = functools.reduce(jnp.maximum, [jnp.max(sc, axis=-1, keepdims=True) for sc in scores])
    mx = mx * scale
    if sink_vals is not None:
        sink = jnp.concatenate([jnp.full((tq, 1), v, F32) for v in sink_vals], axis=0)
        mx = jnp.maximum(mx, sink)
    mx2 = mx * log2e
    den, out = None, None
    for (s, c0, c1), sc in zip(chunks, scores):
        e = jnp.exp2(sc * (scale * log2e) - mx2)
        r = jnp.sum(e, axis=-1, keepdims=True)
        o = jnp.dot(e.astype(BF16), kv[2 * s + 1][c0:c1, :], preferred_element_type=F32)
        den = r if den is None else den + r
        out = o if out is None else out + o
    if sink_vals is not None:
        den = den + jnp.exp2(sink * log2e - mx2)
    out = out / den
    for g in range(group):
        c = o_col0 + g * HEAD_DIM
        o_ref[:, c:c + HEAD_DIM] = out[g * tq:(g + 1) * tq].astype(o_ref.dtype)


def _attn_kernel(*refs, n_kv, n_seg, bands, has_sink, tq, seq, group, kchunk):
    sink_ref = refs[-2] if has_sink else None
    o_ref = refs[-1]
    n = pl.program_id(1)
    for h in range(n_kv):
        kv = refs[n_kv + h * 2 * n_seg:n_kv + (h + 1) * 2 * n_seg]
        sink_vals = [sink_ref[h * group + g] for g in range(group)] if has_sink else None
        _attn_head(refs[h], kv, sink_vals, o_ref, h * group * HEAD_DIM,
                   n=n, bands=bands, tq=tq, seq=seq, group=group, kchunk=kchunk)


def _attention(act, segs, *, n_batch, q_rows0, q_len, q_head0, n_kv, group, tq, sink, name):
    gw = group * HEAD_DIM
    nq = q_len // tq
    assert q_rows0 % tq == 0 and q_len % tq == 0 and q_head0 % group == 0
    in_specs, args = [], []
    for h in range(n_kv):
        in_specs.append(pl.BlockSpec(
            (tq, gw), lambda b, n, h=h: (q_rows0 // tq + b * nq + n, q_head0 // group + h)))
        args.append(act)
    for h in range(n_kv):
        for rows, row_fn, k0, v0, _ in segs:
            for c0 in (k0, v0):
                in_specs.append(pl.BlockSpec(
                    (rows, HEAD_DIM), lambda b, n, row_fn=row_fn, c=c0 + h: (row_fn(b, n), c)))
                args.append(act)
    if sink is not None:
        in_specs.append(pl.BlockSpec(memory_space=pltpu.SMEM))
        args.append(sink)
    body = functools.partial(
        _attn_kernel, n_kv=n_kv, n_seg=len(segs), bands=tuple(s[4] for s in segs),
        has_sink=sink is not None, tq=tq, seq=q_len, group=group, kchunk=512)
    return pl.pallas_call(
        body,
        grid=(n_batch, nq),
        in_specs=in_specs,
        out_specs=pl.BlockSpec((tq, n_kv * gw), lambda b, n: (b * nq + n, 0)),
        out_shape=jax.ShapeDtypeStruct((n_batch * q_len, n_kv * gw), BF16),
        compiler_params=_params("parallel", "arbitrary"),
        name=name,
    )(*args)


def _retention_kernel(lg_ref, *refs, n_batch):
    c = RET_CHUNK
    of_ref, ob_ref, state_ref = refs[-3:]

    @pl.when(pl.program_id(0) == 0)
    def _():
        state_ref[...] = jnp.zeros_like(state_ref)

    row = lax.broadcasted_iota(jnp.int32, (c, c), 0)
    col = lax.broadcasted_iota(jnp.int32, (c, c), 1)
    pos = lax.broadcasted_iota(jnp.int32, (c, 1), 0).astype(F32)
    for d, o_ref in enumerate((of_ref, ob_ref)):
        diff = (row - col) if d == 0 else (col - row)
        keep = (diff >= 0) if d == 0 else (diff > 0)
        dist = jnp.maximum(diff, 0).astype(F32)
        q_pow = (pos + 1.0) if d == 0 else (c - pos)
        k_pow = (c - 1.0 - pos) if d == 0 else pos
        for h in range(C_HEADS):
            cols = slice(h * HEAD_DIM, (h + 1) * HEAD_DIM)
            lg = lg_ref[d, h]
            d_intra = jnp.where(keep, jnp.exp(lg * dist), 0.0)
            d_q = jnp.exp(lg * q_pow)
            d_k = jnp.exp(lg * k_pow)
            d_c = jnp.exp(jnp.full((1, 1), lg * c, F32))
            for b in range(n_batch):
                q_ref, k_ref, kfull_ref, v_ref = refs[(2 * b + d) * 4:(2 * b + d) * 4 + 4]
                q, k, v = q_ref[:, cols], k_ref[:, cols], v_ref[:, cols]
                state = state_ref[b, d, h]
                att = lax.dot_general(q, k, (((1,), (1,)), ((), ())), preferred_element_type=F32) * d_intra
                o = (jnp.dot(att.astype(BF16), v, preferred_element_type=F32)
                     + jnp.dot(q, state.astype(BF16), preferred_element_type=F32) * d_q)
                o_ref[b, :, cols] = o
                kd = (kfull_ref[:, cols] * d_k).T.astype(BF16)
                state_ref[b, d, h] = state * d_c + jnp.dot(kd, v, preferred_element_type=F32)


def _retention(act, krf, log_gamma, *, n_batch, seq, ctx_len):
    c = RET_CHUNK
    w = C_HEADS * HEAD_DIM
    n_ctx, n_lat = ctx_len // c, seq // c
    n_steps = n_ctx + n_lat
    ctx0 = n_batch * seq // c

    def fwd_pos(t):
        return t

    def bwd_pos(t):
        return jnp.where(t < n_ctx, n_ctx - 1 - t, n_steps - 1 - (t - n_ctx))

    def global_chunk(b, p):
        return jnp.where(p < n_ctx, ctx0 + b * n_ctx + p, b * n_lat + p - n_ctx)

    def spec(b, pos_fn, col_block):
        return pl.BlockSpec((c, w), lambda t: (global_chunk(b, pos_fn(t)), col_block))

    qc, kc, vc = _QR * HEAD_DIM // w, _KR * HEAD_DIM // w, _VR * HEAD_DIM // w
    in_specs, args = [pl.BlockSpec(memory_space=pltpu.SMEM)], [log_gamma]
    for b in range(n_batch):
        for pos_fn in (fwd_pos, bwd_pos):
            in_specs += [spec(b, pos_fn, qc), spec(b, pos_fn, kc), spec(b, pos_fn, 0), spec(b, pos_fn, vc)]
            args += [act, act, krf, act]
    out_shape = jax.ShapeDtypeStruct((n_batch, ctx_len + seq, w), F32)
    return pl.pallas_call(
        functools.partial(_retention_kernel, n_batch=n_batch),
        grid=(n_steps,),
        in_specs=in_specs,
        out_specs=[pl.BlockSpec((n_batch, c, w), lambda t: (0, fwd_pos(t), 0)),
                   pl.BlockSpec((n_batch, c, w), lambda t: (0, bwd_pos(t), 0))],
        out_shape=[out_shape, out_shape],
        scratch_shapes=[pltpu.VMEM((n_batch, 2, C_HEADS, HEAD_DIM, HEAD_DIM), F32)],
        compiler_params=_params("arbitrary"),
        name="retention",
    )(*args)


def _gn_gate_kernel(of_ref, ob_ref, grs_ref, g_ref, o_ref):
    for h in range(C_HEADS):
        cols = slice(h * HEAD_DIM, (h + 1) * HEAD_DIM)
        o = of_ref[:, cols] + ob_ref[:, cols]
        mu = jnp.mean(o, axis=-1, keepdims=True)
        var = jnp.mean(jnp.square(o - mu), axis=-1, keepdims=True)
        y = (o - mu) * lax.rsqrt(var + GN_EPS) * g_ref[:, cols]
        o_ref[:, cols] = (grs_ref[:, cols] * y).astype(o_ref.dtype)


def _gn_gate(o_f, o_b, grs, gn_g, layer, m, *, lat_rows, seq, ctx_len):
    w = o_f.shape[-1]
    tm = _tile(ctx_len, 256, 8)

    def scan_block(i):
        r = i * tm
        in_lat = r < lat_rows
        b = jnp.where(in_lat, r // seq, (r - lat_rows) // ctx_len)
        off = jnp.where(in_lat, ctx_len + r % seq, (r - lat_rows) % ctx_len)
        return b, off // tm, 0

    row = pl.BlockSpec((tm, w), lambda i: (i, 0))
    scan = pl.BlockSpec((None, tm, w), scan_block)
    return pl.pallas_call(
        _gn_gate_kernel,
        grid=(m // tm,),
        in_specs=[scan, scan, row, pl.BlockSpec((None, 1, w), lambda i: (layer, 0, 0))],
        out_specs=row,
        out_shape=jax.ShapeDtypeStruct((m, w), BF16),
        compiler_params=_params("parallel"),
        name="gn_gate",
    )(o_f, o_b, grs, gn_g)


def _merge_kernel(ya_ref, yb_ref, yc_ref, w_ref, ga_ref, gb_ref, gc_ref, o_ref, wb_ref):
    @pl.when(pl.program_id(1) == 0)
    def _():
        wb_ref[...] = w_ref[...].astype(BF16)

    out = None
    for i, (y_ref, g_ref) in enumerate(((ya_ref, ga_ref), (yb_ref, gb_ref), (yc_ref, gc_ref))):
        t = g_ref[...].astype(F32) * jnp.dot(y_ref[...], wb_ref[i], preferred_element_type=F32)
        out = t if out is None else out + t
    o_ref[...] = out.astype(o_ref.dtype)


def _merge(y_a, y_b, y_c, gates, w_branch, layer, m):
    d = w_branch.shape[-1]
    bw = w_branch.shape[-2]
    tm = _tile(m, 512, 8)
    tn = _tile(d, 512, 128)
    nj = d // tn
    y_spec = pl.BlockSpec((tm, bw), lambda j, i: (i, 0))
    g_spec = lambda br: pl.BlockSpec((tm, tn), lambda j, i: (i, br * nj + j))
    return pl.pallas_call(
        _merge_kernel,
        grid=(nj, m // tm),
        in_specs=[y_spec, y_spec, y_spec,
                  pl.BlockSpec((None, N_BRANCH, bw, tn), lambda j, i: (layer, 0, 0, j)),
                  g_spec(0), g_spec(1), g_spec(2)],
        out_specs=pl.BlockSpec((tm, tn), lambda j, i: (i, j)),
        out_shape=jax.ShapeDtypeStruct((m, d), BF16),
        scratch_shapes=[pltpu.VMEM((N_BRANCH, bw, tn), BF16)],
        compiler_params=_params("parallel", "arbitrary"),
        name="merge",
    )(y_a, y_b, y_c, w_branch, gates, gates, gates)


def _ln_kernel(lat_ref, y_ref, gate_ref, lng_ref, lnb_ref, *rest, alpha, emit_h):
    z = alpha * lat_ref[...] + gate_ref[...] * y_ref[...]
    mu = jnp.mean(z, axis=-1, keepdims=True)
    var = jnp.mean(jnp.square(z - mu), axis=-1, keepdims=True)
    out = (z - mu) * lax.rsqrt(var + LN_EPS) * lng_ref[...] + lnb_ref[...]
    if emit_h:
        sc_ref, sh_ref, lat_out, h_out = rest
        h_out[...] = (out * (1.0 + sc_ref[...]) + sh_ref[...]).astype(h_out.dtype)
    else:
        lat_out, = rest
    lat_out[...] = out


def _post_ln(lat, y, mod, k_gate, mod_next, k_scale, k_shift, ln_g, ln_b, layer, m, group_of, alpha):
    d = lat.shape[1]
    tm = 256
    emit_h = mod_next is not None
    row = pl.BlockSpec((tm, d), lambda i: (i, 0))
    vec = lambda k: pl.BlockSpec((None, None, 1, d), lambda i: (k, group_of(i * tm), 0, 0))
    par = pl.BlockSpec((None, 1, d), lambda i: (layer, 0, 0))
    in_specs = [row, row, vec(k_gate), par, par]
    args = [lat, y, mod, ln_g, ln_b]
    out_specs = [row]
    out_shape = [jax.ShapeDtypeStruct((m, d), F32)]
    if emit_h:
        in_specs += [vec(k_scale), vec(k_shift)]
        args += [mod_next, mod_next]
        out_specs.append(row)
        out_shape.append(jax.ShapeDtypeStruct((m, d), BF16))
    res = pl.pallas_call(
        functools.partial(_ln_kernel, alpha=alpha, emit_h=emit_h),
        grid=(m // tm,),
        in_specs=in_specs,
        out_specs=out_specs,
        out_shape=out_shape,
        compiler_params=_params("parallel"),
        name="post_ln",
    )(*args)
    return (res[0], res[1]) if emit_h else (res[0], None)


_HALO = 8
_UP_ROW_TILES = 4


def _up_conv_kernel(a_ref, wg_ref, wu_ref, cw_ref, cb_ref, wd_ref, o_ref, wd_out_ref, wb_ref, *,
                    kc, n_col_tiles, tm, n_sub, seq_first_rows, seq_last_rows):
    j, i = pl.program_id(0), pl.program_id(1)
    wd_out_ref[...] = wd_ref[...].astype(BF16)

    @pl.when(j < n_col_tiles)
    def _():
        rows = pl.ds(pl.multiple_of(i * kc, kc), kc)
        wb_ref[j % 2, 0, rows, :] = wg_ref[...].astype(BF16)
        wb_ref[j % 2, 1, rows, :] = wu_ref[...].astype(BF16)

    @pl.when(j > 0)
    def _():
        slot = (j + 1) % 2
        ts = tm // n_sub
        n_rows = ts + 2 * _HALO
        for k in range(n_sub):
            r0 = k * ts
            g = jnp.dot(a_ref[r0:r0 + n_rows, :], wb_ref[slot, 0], preferred_element_type=F32)
            u = jnp.dot(a_ref[r0 + _HALO:r0 + _HALO + ts, :], wb_ref[slot, 1], preferred_element_type=F32)
            g_prev = pltpu.roll(g, 1, 0)[_HALO:_HALO + ts]
            g_next = pltpu.roll(g, n_rows - 1, 0)[_HALO:_HALO + ts]
            row = i * tm + r0 + lax.broadcasted_iota(jnp.int32, (ts, 1), 0)
            has_prev = functools.reduce(lambda x, y: x & y, [row != r for r in seq_first_rows])
            has_next = functools.reduce(lambda x, y: x & y, [row != r for r in seq_last_rows])
            conv = (cb_ref[...] + jnp.where(has_prev, g_prev, 0.0) * cw_ref[0:1, :]
                    + g[_HALO:_HALO + ts] * cw_ref[1:2, :] + jnp.where(has_next, g_next, 0.0) * cw_ref[2:3, :])
            o_ref[r0:r0 + ts, :] = (_silu(conv) * u).astype(o_ref.dtype)


def _up_conv_gate(h, w_up, conv_w, conv_b, w_down, layer, m, seq_lens):
    k = h.shape[1]
    f = w_up.shape[-1] // 2
    d_out = w_down.shape[-1]
    ni = _UP_ROW_TILES
    tm, kc = m // ni, k // ni
    assert m % ni == 0 and tm % 8 == 0 and k % ni == 0 and kc % 16 == 0 and sum(seq_lens) == m
    tf = _tile(f, 256, 128)
    nj = f // tf
    wd_rows = f // (nj * ni)
    assert f % (nj * ni) == 0 and wd_rows % 16 == 0
    wd_blk = lambda j, i: jnp.maximum(j - 1, 0) * ni + jnp.where(j == 0, 0, i)
    starts = [sum(seq_lens[:s]) for s in range(len(seq_lens))]
    first_rows = tuple(starts)
    last_rows = tuple(st + n - 1 for st, n in zip(starts, seq_lens))
    hp = jnp.pad(h[:m], ((_HALO, _HALO), (0, 0)))
    tiles = jnp.stack([hp[t * tm:t * tm + tm + 2 * _HALO] for t in range(ni)])
    row = lambda j, i: jnp.where(j == 0, 0, i)
    col = lambda j: jnp.maximum(j - 1, 0)
    chunk = lambda j, i: jnp.where(j == nj, ni - 1, i)
    stage = lambda j: jnp.minimum(j, nj - 1)
    n_sub = max(1, tm // 512)
    assert tm % (16 * n_sub) == 0
    body = functools.partial(_up_conv_kernel, kc=kc, n_col_tiles=nj, tm=tm, n_sub=n_sub,
                             seq_first_rows=first_rows, seq_last_rows=last_rows)
    return pl.pallas_call(
        body,
        grid=(nj + 1, ni),
        in_specs=[pl.BlockSpec((None, tm + 2 * _HALO, k), lambda j, i: (row(j, i), 0, 0)),
                  pl.BlockSpec((None, kc, tf), lambda j, i: (layer, chunk(j, i), stage(j))),
                  pl.BlockSpec((None, kc, tf), lambda j, i: (layer, chunk(j, i), nj + stage(j))),
                  pl.BlockSpec((None, CONV_WIDTH, tf), lambda j, i: (layer, 0, col(j))),
                  pl.BlockSpec((None, 1, tf), lambda j, i: (layer, 0, col(j))),
                  pl.BlockSpec((None, wd_rows, d_out), lambda j, i: (layer, wd_blk(j, i), 0))],
        out_specs=[pl.BlockSpec((tm, tf), lambda j, i: (row(j, i), col(j))),
                   pl.BlockSpec((wd_rows, d_out), lambda j, i: (wd_blk(j, i), 0))],
        out_shape=[jax.ShapeDtypeStruct((m, f), BF16), jax.ShapeDtypeStruct((f, d_out), BF16)],
        scratch_shapes=[pltpu.VMEM((2, 2, k, tf), BF16)],
        compiler_params=_params("arbitrary", "arbitrary"),
        name="up_conv_gate",
    )(tiles, w_up, w_up, conv_w, conv_b, w_down)


def _rope_tables(seq, n_batch, n_ctx_rows):
    rows = seq // GRID_W
    r_idx, c_idx = jnp.meshgrid(jnp.arange(rows, dtype=F32), jnp.arange(GRID_W, dtype=F32), indexing='ij')
    axis_dim = HEAD_DIM // 2
    inv_freq = ROPE_BASE ** (-jnp.arange(0, axis_dim, 2, dtype=F32) / axis_dim)
    ang_r = r_idx.reshape(-1)[:, None] * inv_freq
    ang_c = c_idx.reshape(-1)[:, None] * inv_freq
    cr, sr, cc, sn = jnp.cos(ang_r), jnp.sin(ang_r), jnp.cos(ang_c), jnp.sin(ang_c)
    zero = jnp.zeros_like(sr)
    c = jnp.concatenate([cr, cr, cc, cc], axis=-1)
    sa = jnp.concatenate([-sr, zero, -sn, zero], axis=-1)
    sb = jnp.concatenate([zero, sr, zero, sn], axis=-1)

    def full(t, fill):
        return jnp.concatenate([jnp.tile(t, (n_batch, 1)), jnp.full((n_ctx_rows, HEAD_DIM), fill, F32)], axis=0)

    return full(c, 1.0), full(sa, 0.0), full(sb, 0.0)


def kernel(x, c, ctx, c_ctx, w_mod, b_mod, w_in, b_gate, w_branch, w_o, attn_sink, q_norm_g, k_norm_g,
           ret_decay, ret_gn_g, ln1_g, ln1_b, w_up, ffn_conv_w, ffn_conv_b, w_down, ln2_g, ln2_b):
    n_batch, seq, d = x.shape
    ctx_len = ctx.shape[1]
    depth = w_mod.shape[0]
    d_ff = w_down.shape[1]
    lat_rows = n_batch * seq
    m_all = lat_rows + n_batch * ctx_len
    alpha = (2 * depth) ** 0.25
    qkv_w = _N_HEADS_QKV * HEAD_DIM
    assert seq % 256 == 0 and ctx_len % 256 == 0 and lat_rows % ctx_len == 0

    def group_of(row_start):
        return jnp.minimum(row_start // seq, n_batch)

    c_rows = jnp.concatenate([c, c_ctx[None], jnp.zeros((8 - n_batch - 1, d), F32)], axis=0)
    mod = _modulation(c_rows, w_mod, b_mod)
    mod = mod.reshape(depth, 8, 6, d).transpose(0, 2, 1, 3)[:, :, :, None, :]
    SH1, SC1, G1, SH2, SC2, G2 = range(6)

    rope_c, rope_sa, rope_sb = _rope_tables(seq, n_batch, n_batch * ctx_len)
    log_gamma = -jnp.exp(ret_decay.astype(F32))
    b_gate3 = b_gate.reshape(depth, 1, -1)
    gn_g3 = ret_gn_g.reshape(depth, 1, -1)
    ln1_g3, ln1_b3 = ln1_g.reshape(depth, 1, d), ln1_b.reshape(depth, 1, d)
    ln2_g3, ln2_b3 = ln2_g.reshape(depth, 1, d), ln2_b.reshape(depth, 1, d)
    conv_b3 = ffn_conv_b.reshape(depth, 1, d_ff)

    lat = jnp.concatenate([x.reshape(lat_rows, d), ctx.reshape(n_batch * ctx_len, d)], axis=0)
    h = _modulate(lat, mod[0], SC1, SH1, group_of)

    for l in range(depth):
        last = l == depth - 1
        m = lat_rows if last else m_all

        p = _matmul(h, w_in, l, 0, qkv_w, m_all, F32, name="in_proj_qkv")
        gates = _matmul(h, w_in, l, qkv_w, N_BRANCH * d, m, BF16, bias=b_gate3, name="in_proj_gate")
        act, krf, grs = _prep(p, rope_c, rope_sa, rope_sb, q_norm_g[l][None], k_norm_g[l][None])

        tq = 256
        r128 = tq // 128
        lat128 = seq // 128
        ctx_seg = lambda k0, v0: (ctx_len, lambda b, n: lat_rows // ctx_len + b, k0, v0, None)
        segs_a = [
            (128, lambda b, n: b * lat128 + jnp.maximum(n * r128 - 1, 0), _KA, _VA, -128),
            (tq, lambda b, n: b * (seq // tq) + n, _KA, _VA, 0),
            (128, lambda b, n: b * lat128 + jnp.minimum((n + 1) * r128, lat128 - 1), _KA, _VA, tq),
            ctx_seg(_KA, _VA),
        ]
        attn = functools.partial(_attention, act, n_batch=n_batch)
        y_a = attn(segs_a, q_rows0=0, q_len=seq, q_head0=_QA, n_kv=A_KV_HEADS,
                   group=A_HEADS // A_KV_HEADS, tq=tq, sink=attn_sink[l], name="attn_window")
        segs_b = [ctx_seg(_KB, _VB), (seq, lambda b, n: b, _KB, _VB, None)]
        y_b = attn(segs_b, q_rows0=0, q_len=seq, q_head0=_QB, n_kv=B_KV_HEADS,
                   group=B_HEADS // B_KV_HEADS, tq=128, sink=None, name="attn_dense")
        if not last:
            y_a_c = attn([ctx_seg(_KA, _VA)], q_rows0=lat_rows, q_len=ctx_len, q_head0=_QA,
                         n_kv=A_KV_HEADS, group=A_HEADS // A_KV_HEADS, tq=ctx_len, sink=attn_sink[l],
                         name="attn_ctx_a")
            y_b_c = attn([ctx_seg(_KB, _VB)], q_rows0=lat_rows, q_len=ctx_len, q_head0=_QB,
                         n_kv=B_KV_HEADS, group=B_HEADS // B_KV_HEADS, tq=ctx_len, sink=None,
                         name="attn_ctx_b")
            y_a = jnp.concatenate([y_a, y_a_c], axis=0)
            y_b = jnp.concatenate([y_b, y_b_c], axis=0)

        o_f, o_b = _retention(act, krf, log_gamma[l], n_batch=n_batch, seq=seq, ctx_len=ctx_len)
        y_c = _gn_gate(o_f, o_b, grs, gn_g3, l, m, lat_rows=lat_rows, seq=seq, ctx_len=ctx_len)

        merged = _merge(y_a, y_b, y_c, gates, w_branch, l, m)
        y = _matmul(merged, w_o, l, 0, d, m, F32, name="out_proj")
        lat, h2 = _post_ln(lat, y, mod[l], G1, mod[l], SC2, SH2, ln1_g3, ln1_b3, l, m, group_of, alpha)

        seq_lens = [seq] * n_batch + ([] if last else [ctx_len] * n_batch)
        a, w_down_bf = _up_conv_gate(h2, w_up, ffn_conv_w, conv_b3, w_down, l, m, seq_lens)
        y2 = _matmul_a_resident(a, w_down_bf, m, F32, name="down_proj")
        nxt = None if last else mod[l + 1]
        lat, h = _post_ln(lat, y2, mod[l], G2, nxt, SC1, SH1, ln2_g3, ln2_b3, l, m, group_of, alpha)

    return lat.reshape(n_batch, seq, d)
```

```python
import functools
import math

import jax
import jax.numpy as jnp
from jax import lax
from jax.experimental import pallas as pl
from jax.experimental.pallas import tpu as pltpu

F32 = jnp.float32
BF16 = jnp.bfloat16

GRID_W = 64
HEAD_DIM = 128
A_HEADS = 8
A_KV_HEADS = 2
B_HEADS = 8
B_KV_HEADS = 2
C_HEADS = 8
WINDOW = 128
RET_CHUNK = 128
N_BRANCH = 3
CONV_WIDTH = 3
ROPE_BASE = 10000.0
LN_EPS = 1e-5
RMS_EPS = 1e-6
GN_EPS = 1e-5
NEG_INF = -1e30

V7X_VMEM_BYTES = 64 * 1024 * 1024
VMEM_LIMIT = V7X_VMEM_BYTES - 8 * 1024 * 1024

_QA, _KA, _VA = 0, 8, 10
_QB, _KB, _VB = 12, 20, 22
_QR, _KR, _VR, _GR = 24, 32, 40, 48
_N_HEADS_QKV = 56
_N_HEADS_ACT = 48


def _params(*sem):
    return pltpu.CompilerParams(dimension_semantics=sem, vmem_limit_bytes=VMEM_LIMIT)


def _tile(n, pref, unit):
    t = min(pref, n) // unit * unit
    while t > unit and n % t:
        t -= unit
    assert t >= unit and n % t == 0, (n, pref, unit)
    return t


def _sigmoid(x):
    return 1.0 / (1.0 + jnp.exp(-x))


def _silu(x):
    return x * _sigmoid(x)


def _mod_kernel(c_ref, w_ref, b_ref, o_ref):
    a = _silu(c_ref[...]).astype(BF16)
    o_ref[...] = jnp.dot(a, w_ref[...].astype(BF16), preferred_element_type=F32) + b_ref[...]


def _modulation(c_rows, w_mod, b_mod):
    depth, d, n = w_mod.shape
    rows = c_rows.shape[0]
    tn = _tile(n, 1024, 128)
    return pl.pallas_call(
        _mod_kernel,
        grid=(depth, n // tn),
        in_specs=[pl.BlockSpec((rows, d), lambda l, j: (0, 0)),
                  pl.BlockSpec((None, d, tn), lambda l, j: (l, 0, j)),
                  pl.BlockSpec((None, 1, tn), lambda l, j: (l, 0, j))],
        out_specs=pl.BlockSpec((None, rows, tn), lambda l, j: (l, 0, j)),
        out_shape=jax.ShapeDtypeStruct((depth, rows, n), F32),
        compiler_params=_params("parallel", "parallel"),
        name="modulation",
    )(c_rows, w_mod, b_mod.reshape(depth, 1, n))


def _modulate_kernel(x_ref, sc_ref, sh_ref, o_ref):
    o_ref[...] = (x_ref[...] * (1.0 + sc_ref[...]) + sh_ref[...]).astype(o_ref.dtype)


def _modulate(x, mod, k_scale, k_shift, group_of):
    m, d = x.shape
    tm = 256
    vec = lambda k: pl.BlockSpec((None, None, 1, d), lambda i: (k, group_of(i * tm), 0, 0))
    return pl.pallas_call(
        _modulate_kernel,
        grid=(m // tm,),
        in_specs=[pl.BlockSpec((tm, d), lambda i: (i, 0)), vec(k_scale), vec(k_shift)],
        out_specs=pl.BlockSpec((tm, d), lambda i: (i, 0)),
        out_shape=jax.ShapeDtypeStruct((m, d), BF16),
        compiler_params=_params("parallel"),
        name="modulate",
    )(x, mod, mod)


def _mm_kernel(a_ref, w_ref, *rest, kc, n_col_tiles, has_bias):
    b_ref = rest[0] if has_bias else None
    o_ref, wb_ref = rest[-2], rest[-1]
    j, i = pl.program_id(0), pl.program_id(1)

    @pl.when(j < n_col_tiles)
    def _():
        wb_ref[j % 2, pl.ds(pl.multiple_of(i * kc, kc), kc), :] = w_ref[...].astype(BF16)

    @pl.when(j > 0)
    def _():
        acc = jnp.dot(a_ref[...], wb_ref[(j + 1) % 2], preferred_element_type=F32)
        if has_bias:
            acc = _sigmoid(acc + b_ref[...])
        o_ref[...] = acc.astype(o_ref.dtype)


def _mm_plain_kernel(a_ref, w_ref, o_ref):
    o_ref[...] = jnp.dot(a_ref[...], w_ref[...], preferred_element_type=F32).astype(o_ref.dtype)


def _matmul_a_resident(a, w, m, out_dtype, *, tm=512, tn=512, name="matmul"):
    k = a.shape[1]
    n = w.shape[-1]
    tm = _tile(m, tm, 8)
    tn = _tile(n, tn, 128)
    return pl.pallas_call(
        _mm_plain_kernel,
        grid=(m // tm, n // tn),
        in_specs=[pl.BlockSpec((tm, k), lambda i, j: (i, 0)),
                  pl.BlockSpec((k, tn), lambda i, j: (0, j))],
        out_specs=pl.BlockSpec((tm, tn), lambda i, j: (i, j)),
        out_shape=jax.ShapeDtypeStruct((m, n), out_dtype),
        compiler_params=_params("parallel", "arbitrary"),
        name=name,
    )(a, w)


_MM_ROW_TILES = 8


def _matmul(a, w, layer, n0, n, m, out_dtype, *, tn=1024, bias=None, name="matmul"):
    k = a.shape[1]
    ni = _MM_ROW_TILES
    tm, kc = m // ni, k // ni
    assert m % ni == 0 and tm % 8 == 0 and k % ni == 0 and kc % 16 == 0
    tn = _tile(math.gcd(n, n0) if n0 else n, tn, 128)
    j0, nj = n0 // tn, n // tn
    row = lambda j, i: jnp.where(j == 0, 0, i)
    col = lambda j: jnp.maximum(j - 1, 0)
    in_specs = [pl.BlockSpec((tm, k), lambda j, i: (row(j, i), 0)),
                pl.BlockSpec((None, kc, tn),
                             lambda j, i: (layer, jnp.where(j == nj, ni - 1, i), jnp.minimum(j, nj - 1) + j0))]
    args = [a, w]
    if bias is not None:
        in_specs.append(pl.BlockSpec((None, 1, tn), lambda j, i: (layer, 0, col(j))))
        args.append(bias)
    return pl.pallas_call(
        functools.partial(_mm_kernel, kc=kc, n_col_tiles=nj, has_bias=bias is not None),
        grid=(nj + 1, ni),
        in_specs=in_specs,
        out_specs=pl.BlockSpec((tm, tn), lambda j, i: (row(j, i), col(j))),
        out_shape=jax.ShapeDtypeStruct((m, n), out_dtype),
        scratch_shapes=[pltpu.VMEM((2, k, tn), BF16)],
        compiler_params=_params("arbitrary", "arbitrary"),
        name=name,
    )(*args)


def _rope(x, c, sa, sb):
    return x * c + pltpu.roll(x, 96, 1) * sa + pltpu.roll(x, 32, 1) * sb


def _rms(x, g):
    return x * lax.rsqrt(jnp.mean(x * x, axis=-1, keepdims=True) + RMS_EPS) * g


def _prep_kernel(p_ref, c_ref, sa_ref, sb_ref, qg_ref, kg_ref, act_ref, krf_ref, grs_ref):
    c, sa, sb = c_ref[...], sa_ref[...], sb_ref[...]
    qg, kg = qg_ref[...], kg_ref[...]
    k_scale = HEAD_DIM ** -0.5
    for h in range(_N_HEADS_QKV):
        cols = slice(h * HEAD_DIM, (h + 1) * HEAD_DIM)
        x = p_ref[:, cols]
        if h < _KA + A_KV_HEADS:
            y = _rope(x, c, sa, sb)
        elif h < _QB:
            y = x
        elif h < _KB:
            y = _rope(_rms(x, qg), c, sa, sb)
        elif h < _VB:
            y = _rope(_rms(x, kg), c, sa, sb)
        elif h < _QR:
            y = x
        elif h < _KR:
            y = _rope(x, c, sa, sb)
        elif h < _VR:
            y = _rope(x, c, sa, sb) * k_scale
            krf_ref[:, (h - _KR) * HEAD_DIM:(h - _KR + 1) * HEAD_DIM] = y
        elif h < _GR:
            y = x
        else:
            grs_ref[:, (h - _GR) * HEAD_DIM:(h - _GR + 1) * HEAD_DIM] = _silu(x)
            continue
        act_ref[:, cols] = y.astype(BF16)


def _prep(p, rope_c, rope_sa, rope_sb, qg, kg):
    m = p.shape[0]
    tm = 256
    row = lambda w: pl.BlockSpec((tm, w), lambda i: (i, 0))
    vec = pl.BlockSpec((1, HEAD_DIM), lambda i: (0, 0))
    return pl.pallas_call(
        _prep_kernel,
        grid=(m // tm,),
        in_specs=[row(_N_HEADS_QKV * HEAD_DIM), row(HEAD_DIM), row(HEAD_DIM), row(HEAD_DIM), vec, vec],
        out_specs=[row(_N_HEADS_ACT * HEAD_DIM), row(C_HEADS * HEAD_DIM), row(C_HEADS * HEAD_DIM)],
        out_shape=[jax.ShapeDtypeStruct((m, _N_HEADS_ACT * HEAD_DIM), BF16),
                   jax.ShapeDtypeStruct((m, C_HEADS * HEAD_DIM), F32),
                   jax.ShapeDtypeStruct((m, C_HEADS * HEAD_DIM), F32)],
        compiler_params=_params("parallel"),
        name="prep",
    )(p, rope_c, rope_sa, rope_sb, qg, kg)


def _attn_head(q_ref, kv, sink_vals, o_ref, o_col0, *, n, bands, tq, seq, group, kchunk, online):
    n_seg = len(bands)
    scale = HEAD_DIM ** -0.5
    log2e = math.log2(math.e)
    q = jnp.concatenate([q_ref[:, g * HEAD_DIM:(g + 1) * HEAD_DIM] for g in range(group)], axis=0)

    def raw_scores(s, c0, c1):
        sc = lax.dot_general(q, kv[2 * s][c0:c1, :], (((1,), (1,)), ((), ())), preferred_element_type=F32)
        if bands[s] is not None:
            qpos = n * tq + lax.broadcasted_iota(jnp.int32, (tq, c1 - c0), 0)
            kpos = n * tq + (bands[s] + c0) + lax.broadcasted_iota(jnp.int32, (tq, c1 - c0), 1)
            valid = (jnp.abs(qpos - kpos) <= WINDOW) & (kpos >= 0) & (kpos < seq)
            sc = jnp.where(jnp.concatenate([valid] * group, axis=0), sc, NEG_INF)
        return sc

    chunks = [(s, c0, min(c0 + kchunk, kv[2 * s].shape[0]))
              for s in range(n_seg) for c0 in range(0, kv[2 * s].shape[0], kchunk)]
    if online:
        c2 = scale * log2e
        sink2 = None
        if sink_vals is not None:
            sink2 = jnp.concatenate([jnp.full((tq, 1), v, F32) for v in sink_vals], axis=0) * log2e
        m2, den, out = sink2, None, None
        for s, c0, c1 in chunks:
            sc = raw_scores(s, c0, c1)
            cm = jnp.max(sc, axis=-1, keepdims=True) * c2
            m_new = cm if m2 is None else jnp.maximum(m2, cm)
            e = jnp.exp2(sc * c2 - m_new)
            r = jnp.sum(e, axis=-1, keepdims=True)
            o = jnp.dot(e.astype(BF16), kv[2 * s + 1][c0:c1, :], preferred_element_type=F32)
            if den is None:
                den, out = r, o
            else:
                alpha = jnp.exp2(m2 - m_new)
                den, out = den * alpha + r, out * alpha + o
            m2 = m_new
        if sink2 is not None:
            den = den + jnp.exp2(sink2 - m2)
        out = out / den
        for g in range(group):
            c = o_col0 + g * HEAD_DIM
            o_ref[:, c:c + HEAD_DIM] = out[g * tq:(g + 1) * tq].astype(o_ref.dtype)
        return

    scores = [raw_scores(*ch) for ch in chunks]
    mx = functools.reduce(jnp.maximum, [jnp.max(sc, axis=-1, keepdims=True) for sc in scores])
    mx = mx * scale
    if sink_vals is not None:
        sink = jnp.concatenate([jnp.full((tq, 1), v, F32) for v in sink_vals], axis=0)
        mx = jnp.maximum(mx, sink)
    mx2 = mx * log2e
    den, out = None, None
    for (s, c0, c1), sc in zip(chunks, scores):
        e = jnp.exp2(sc * (scale * log2e) - mx2)
        r = jnp.sum(e, axis=-1, keepdims=True)
        o = jnp.dot(e.astype(BF16), kv[2 * s + 1][c0:c1, :], preferred_element_type=F32)
        den = r if den is None else den + r
        out = o if out is None else out + o
    if sink_vals is not None:
        den = den + jnp.exp2(sink * log2e - mx2)
    out = out / den
    for g in range(group):
        c = o_col0 + g * HEAD_DIM
        o_ref[:, c:c + HEAD_DIM] = out[g * tq:(g + 1) * tq].astype(o_ref.dtype)


def _attn_kernel(*refs, n_kv, n_seg, bands, has_sink, tq, seq, group, kchunk, online):
    sink_ref = refs[-2] if has_sink else None
    o_ref = refs[-1]
    n = pl.program_id(1)
    for h in range(n_kv):
        kv = refs[n_kv + h * 2 * n_seg:n_kv + (h + 1) * 2 * n_seg]
        sink_vals = [sink_ref[h * group + g] for g in range(group)] if has_sink else None
        _attn_head(refs[h], kv, sink_vals, o_ref, h * group * HEAD_DIM,
                   n=n, bands=bands, tq=tq, seq=seq, group=group, kchunk=kchunk, online=online)


def _attention(act, segs, *, n_batch, q_rows0, q_len, q_head0, n_kv, group, tq, sink, name, online=False):
    gw = group * HEAD_DIM
    nq = q_len // tq
    assert q_rows0 % tq == 0 and q_len % tq == 0 and q_head0 % group == 0
    in_specs, args = [], []
    for h in range(n_kv):
        in_specs.append(pl.BlockSpec(
            (tq, gw), lambda b, n, h=h: (q_rows0 // tq + b * nq + n, q_head0 // group + h)))
        args.append(act)
    for h in range(n_kv):
        for rows, row_fn, k0, v0, _ in segs:
            for c0 in (k0, v0):
                in_specs.append(pl.BlockSpec(
                    (rows, HEAD_DIM), lambda b, n, row_fn=row_fn, c=c0 + h: (row_fn(b, n), c)))
                args.append(act)
    if sink is not None:
        in_specs.append(pl.BlockSpec(memory_space=pltpu.SMEM))
        args.append(sink)
    body = functools.partial(
        _attn_kernel, n_kv=n_kv, n_seg=len(segs), bands=tuple(s[4] for s in segs),
        has_sink=sink is not None, tq=tq, seq=q_len, group=group, kchunk=256 if online else 512, online=online)
    return pl.pallas_call(
        body,
        grid=(n_batch, nq),
        in_specs=in_specs,
        out_specs=pl.BlockSpec((tq, n_kv * gw), lambda b, n: (b * nq + n, 0)),
        out_shape=jax.ShapeDtypeStruct((n_batch * q_len, n_kv * gw), BF16),
        compiler_params=_params("parallel", "arbitrary"),
        name=name,
    )(*args)


def _retention_kernel(lg_ref, *refs, n_batch):
    c = RET_CHUNK
    of_ref, ob_ref, state_ref = refs[-3:]

    @pl.when(pl.program_id(0) == 0)
    def _():
        state_ref[...] = jnp.zeros_like(state_ref)

    row = lax.broadcasted_iota(jnp.int32, (c, c), 0)
    col = lax.broadcasted_iota(jnp.int32, (c, c), 1)
    pos = lax.broadcasted_iota(jnp.int32, (c, 1), 0).astype(F32)
    for d, o_ref in enumerate((of_ref, ob_ref)):
        diff = (row - col) if d == 0 else (col - row)
        keep = (diff >= 0) if d == 0 else (diff > 0)
        dist = jnp.maximum(diff, 0).astype(F32)
        q_pow = (pos + 1.0) if d == 0 else (c - pos)
        k_pow = (c - 1.0 - pos) if d == 0 else pos
        for h in range(C_HEADS):
            cols = slice(h * HEAD_DIM, (h + 1) * HEAD_DIM)
            lg = lg_ref[d, h]
            d_intra = jnp.where(keep, jnp.exp(lg * dist), 0.0)
            d_q = jnp.exp(lg * q_pow)
            d_k = jnp.exp(lg * k_pow)
            d_c = jnp.exp(jnp.full((1, 1), lg * c, F32))
            for b in range(n_batch):
                q_ref, k_ref, kfull_ref, v_ref = refs[(2 * b + d) * 4:(2 * b + d) * 4 + 4]
                q, k, v = q_ref[:, cols], k_ref[:, cols], v_ref[:, cols]
                state = state_ref[b, d, h]
                att = lax.dot_general(q, k, (((1,), (1,)), ((), ())), preferred_element_type=F32) * d_intra
                o = (jnp.dot(att.astype(BF16), v, preferred_element_type=F32)
                     + jnp.dot(q, state.astype(BF16), preferred_element_type=F32) * d_q)
                o_ref[b, :, cols] = o
                kd = (kfull_ref[:, cols] * d_k).T.astype(BF16)
                state_ref[b, d, h] = state * d_c + jnp.dot(kd, v, preferred_element_type=F32)


def _retention(act, krf, log_gamma, *, n_batch, seq, ctx_len):
    c = RET_CHUNK
    w = C_HEADS * HEAD_DIM
    n_ctx, n_lat = ctx_len // c, seq // c
    n_steps = n_ctx + n_lat
    ctx0 = n_batch * seq // c

    def fwd_pos(t):
        return t

    def bwd_pos(t):
        return jnp.where(t < n_ctx, n_ctx - 1 - t, n_steps - 1 - (t - n_ctx))

    def global_chunk(b, p):
        return jnp.where(p < n_ctx, ctx0 + b * n_ctx + p, b * n_lat + p - n_ctx)

    def spec(b, pos_fn, col_block):
        return pl.BlockSpec((c, w), lambda t: (global_chunk(b, pos_fn(t)), col_block))

    qc, kc, vc = _QR * HEAD_DIM // w, _KR * HEAD_DIM // w, _VR * HEAD_DIM // w
    in_specs, args = [pl.BlockSpec(memory_space=pltpu.SMEM)], [log_gamma]
    for b in range(n_batch):
        for pos_fn in (fwd_pos, bwd_pos):
            in_specs += [spec(b, pos_fn, qc), spec(b, pos_fn, kc), spec(b, pos_fn, 0), spec(b, pos_fn, vc)]
            args += [act, act, krf, act]
    out_shape = jax.ShapeDtypeStruct((n_batch, ctx_len + seq, w), F32)
    return pl.pallas_call(
        functools.partial(_retention_kernel, n_batch=n_batch),
        grid=(n_steps,),
        in_specs=in_specs,
        out_specs=[pl.BlockSpec((n_batch, c, w), lambda t: (0, fwd_pos(t), 0)),
                   pl.BlockSpec((n_batch, c, w), lambda t: (0, bwd_pos(t), 0))],
        out_shape=[out_shape, out_shape],
        scratch_shapes=[pltpu.VMEM((n_batch, 2, C_HEADS, HEAD_DIM, HEAD_DIM), F32)],
        compiler_params=_params("arbitrary"),
        name="retention",
    )(*args)


def _gn_gate_kernel(of_ref, ob_ref, grs_ref, g_ref, o_ref):
    for h in range(C_HEADS):
        cols = slice(h * HEAD_DIM, (h + 1) * HEAD_DIM)
        o = of_ref[:, cols] + ob_ref[:, cols]
        mu = jnp.mean(o, axis=-1, keepdims=True)
        var = jnp.mean(jnp.square(o - mu), axis=-1, keepdims=True)
        y = (o - mu) * lax.rsqrt(var + GN_EPS) * g_ref[:, cols]
        o_ref[:, cols] = (grs_ref[:, cols] * y).astype(o_ref.dtype)


def _gn_gate(o_f, o_b, grs, gn_g, layer, m, *, lat_rows, seq, ctx_len):
    w = o_f.shape[-1]
    tm = _tile(ctx_len, 256, 8)

    def scan_block(i):
        r = i * tm
        in_lat = r < lat_rows
        b = jnp.where(in_lat, r // seq, (r - lat_rows) // ctx_len)
        off = jnp.where(in_lat, ctx_len + r % seq, (r - lat_rows) % ctx_len)
        return b, off // tm, 0

    row = pl.BlockSpec((tm, w), lambda i: (i, 0))
    scan = pl.BlockSpec((None, tm, w), scan_block)
    return pl.pallas_call(
        _gn_gate_kernel,
        grid=(m // tm,),
        in_specs=[scan, scan, row, pl.BlockSpec((None, 1, w), lambda i: (layer, 0, 0))],
        out_specs=row,
        out_shape=jax.ShapeDtypeStruct((m, w), BF16),
        compiler_params=_params("parallel"),
        name="gn_gate",
    )(o_f, o_b, grs, gn_g)


def _merge_kernel(ya_ref, yb_ref, yc_ref, w_ref, ga_ref, gb_ref, gc_ref, o_ref, wb_ref, *, kc, n_col_tiles):
    j, i = pl.program_id(0), pl.program_id(1)

    @pl.when(j < n_col_tiles)
    def _():
        wb_ref[j % 2, :, pl.ds(pl.multiple_of(i * kc, kc), kc), :] = w_ref[...].astype(BF16)

    @pl.when(j > 0)
    def _():
        slot = (j + 1) % 2
        out = None
        for br, (y_ref, g_ref) in enumerate(((ya_ref, ga_ref), (yb_ref, gb_ref), (yc_ref, gc_ref))):
            t = g_ref[...].astype(F32) * jnp.dot(y_ref[...], wb_ref[slot, br], preferred_element_type=F32)
            out = t if out is None else out + t
        o_ref[...] = out.astype(o_ref.dtype)


def _merge(y_a, y_b, y_c, gates, w_branch, layer, m):
    d = w_branch.shape[-1]
    bw = w_branch.shape[-2]
    ni = _MM_ROW_TILES
    tm, kc = m // ni, bw // ni
    assert m % ni == 0 and tm % 8 == 0 and bw % ni == 0 and kc % 16 == 0
    tn = _tile(d, 1024, 128)
    nj = d // tn
    row = lambda j, i: jnp.where(j == 0, 0, i)
    col = lambda j: jnp.maximum(j - 1, 0)
    y_spec = pl.BlockSpec((tm, bw), lambda j, i: (row(j, i), 0))
    g_spec = lambda br: pl.BlockSpec((tm, tn), lambda j, i: (row(j, i), br * nj + col(j)))
    return pl.pallas_call(
        functools.partial(_merge_kernel, kc=kc, n_col_tiles=nj),
        grid=(nj + 1, ni),
        in_specs=[y_spec, y_spec, y_spec,
                  pl.BlockSpec((None, N_BRANCH, kc, tn),
                               lambda j, i: (layer, 0, jnp.where(j == nj, ni - 1, i), jnp.minimum(j, nj - 1))),
                  g_spec(0), g_spec(1), g_spec(2)],
        out_specs=pl.BlockSpec((tm, tn), lambda j, i: (row(j, i), col(j))),
        out_shape=jax.ShapeDtypeStruct((m, d), BF16),
        scratch_shapes=[pltpu.VMEM((2, N_BRANCH, bw, tn), BF16)],
        compiler_params=_params("arbitrary", "arbitrary"),
        name="merge",
    )(y_a, y_b, y_c, w_branch, gates, gates, gates)


def _ln_kernel(lat_ref, y_ref, gate_ref, lng_ref, lnb_ref, *rest, alpha, emit_h):
    z = alpha * lat_ref[...] + gate_ref[...] * y_ref[...]
    mu = jnp.mean(z, axis=-1, keepdims=True)
    var = jnp.mean(jnp.square(z - mu), axis=-1, keepdims=True)
    out = (z - mu) * lax.rsqrt(var + LN_EPS) * lng_ref[...] + lnb_ref[...]
    if emit_h:
        sc_ref, sh_ref, lat_out, h_out = rest
        h_out[...] = (out * (1.0 + sc_ref[...]) + sh_ref[...]).astype(h_out.dtype)
    else:
        lat_out, = rest
    lat_out[...] = out


def _post_ln(lat, y, mod, k_gate, mod_next, k_scale, k_shift, ln_g, ln_b, layer, m, group_of, alpha):
    d = lat.shape[1]
    tm = 256
    emit_h = mod_next is not None
    row = pl.BlockSpec((tm, d), lambda i: (i, 0))
    vec = lambda k: pl.BlockSpec((None, None, 1, d), lambda i: (k, group_of(i * tm), 0, 0))
    par = pl.BlockSpec((None, 1, d), lambda i: (layer, 0, 0))
    in_specs = [row, row, vec(k_gate), par, par]
    args = [lat, y, mod, ln_g, ln_b]
    out_specs = [row]
    out_shape = [jax.ShapeDtypeStruct((m, d), F32)]
    if emit_h:
        in_specs += [vec(k_scale), vec(k_shift)]
        args += [mod_next, mod_next]
        out_specs.append(row)
        out_shape.append(jax.ShapeDtypeStruct((m, d), BF16))
    res = pl.pallas_call(
        functools.partial(_ln_kernel, alpha=alpha, emit_h=emit_h),
        grid=(m // tm,),
        in_specs=in_specs,
        out_specs=out_specs,
        out_shape=out_shape,
        compiler_params=_params("parallel"),
        name="post_ln",
    )(*args)
    return (res[0], res[1]) if emit_h else (res[0], None)


_HALO = 8
_UP_ROW_TILES = 4


def _up_conv_kernel(a_ref, wg_ref, wu_ref, cw_ref, cb_ref, wd_ref, o_ref, wd_out_ref, wb_ref, *,
                    kc, n_col_tiles, tm, n_sub, seq_first_rows, seq_last_rows):
    j, i = pl.program_id(0), pl.program_id(1)
    wd_out_ref[...] = wd_ref[...].astype(BF16)

    @pl.when(j < n_col_tiles)
    def _():
        rows = pl.ds(pl.multiple_of(i * kc, kc), kc)
        wb_ref[j % 2, 0, rows, :] = wg_ref[...].astype(BF16)
        wb_ref[j % 2, 1, rows, :] = wu_ref[...].astype(BF16)

    @pl.when(j > 0)
    def _():
        slot = (j + 1) % 2
        ts = tm // n_sub
        n_rows = ts + 2 * _HALO
        for k in range(n_sub):
            r0 = k * ts
            g = jnp.dot(a_ref[r0:r0 + n_rows, :], wb_ref[slot, 0], preferred_element_type=F32)
            u = jnp.dot(a_ref[r0 + _HALO:r0 + _HALO + ts, :], wb_ref[slot, 1], preferred_element_type=F32)
            g_prev = pltpu.roll(g, 1, 0)[_HALO:_HALO + ts]
            g_next = pltpu.roll(g, n_rows - 1, 0)[_HALO:_HALO + ts]
            row = i * tm + r0 + lax.broadcasted_iota(jnp.int32, (ts, 1), 0)
            has_prev = functools.reduce(lambda x, y: x & y, [row != r for r in seq_first_rows])
            has_next = functools.reduce(lambda x, y: x & y, [row != r for r in seq_last_rows])
            conv = (cb_ref[...] + jnp.where(has_prev, g_prev, 0.0) * cw_ref[0:1, :]
                    + g[_HALO:_HALO + ts] * cw_ref[1:2, :] + jnp.where(has_next, g_next, 0.0) * cw_ref[2:3, :])
            o_ref[r0:r0 + ts, :] = (_silu(conv) * u).astype(o_ref.dtype)


def _up_conv_gate(h, w_up, conv_w, conv_b, w_down, layer, m, seq_lens):
    k = h.shape[1]
    f = w_up.shape[-1] // 2
    d_out = w_down.shape[-1]
    ni = _UP_ROW_TILES
    tm, kc = m // ni, k // ni
    assert m % ni == 0 and tm % 8 == 0 and k % ni == 0 and kc % 16 == 0 and sum(seq_lens) == m
    tf = _tile(f, 256, 128)
    nj = f // tf
    wd_rows = f // (nj * ni)
    assert f % (nj * ni) == 0 and wd_rows % 16 == 0
    wd_blk = lambda j, i: jnp.maximum(j - 1, 0) * ni + jnp.where(j == 0, 0, i)
    starts = [sum(seq_lens[:s]) for s in range(len(seq_lens))]
    first_rows = tuple(starts)
    last_rows = tuple(st + n - 1 for st, n in zip(starts, seq_lens))
    edge = jnp.zeros((_HALO, k), h.dtype)
    pieces = []
    for t in range(ni):
        lo, hi = t * tm, (t + 1) * tm
        pieces += [h[lo - _HALO:lo] if t else edge, h[lo:hi], h[hi:hi + _HALO] if hi < m else edge]
    tiles = jnp.concatenate(pieces, axis=0).reshape(ni, tm + 2 * _HALO, k)
    row = lambda j, i: jnp.where(j == 0, 0, i)
    col = lambda j: jnp.maximum(j - 1, 0)
    chunk = lambda j, i: jnp.where(j == nj, ni - 1, i)
    stage = lambda j: jnp.minimum(j, nj - 1)
    n_sub = max(1, tm // 512)
    assert tm % (16 * n_sub) == 0
    body = functools.partial(_up_conv_kernel, kc=kc, n_col_tiles=nj, tm=tm, n_sub=n_sub,
                             seq_first_rows=first_rows, seq_last_rows=last_rows)
    return pl.pallas_call(
        body,
        grid=(nj + 1, ni),
        in_specs=[pl.BlockSpec((None, tm + 2 * _HALO, k), lambda j, i: (row(j, i), 0, 0)),
                  pl.BlockSpec((None, kc, tf), lambda j, i: (layer, chunk(j, i), stage(j))),
                  pl.BlockSpec((None, kc, tf), lambda j, i: (layer, chunk(j, i), nj + stage(j))),
                  pl.BlockSpec((None, CONV_WIDTH, tf), lambda j, i: (layer, 0, col(j))),
                  pl.BlockSpec((None, 1, tf), lambda j, i: (layer, 0, col(j))),
                  pl.BlockSpec((None, wd_rows, d_out), lambda j, i: (layer, wd_blk(j, i), 0))],
        out_specs=[pl.BlockSpec((tm, tf), lambda j, i: (row(j, i), col(j))),
                   pl.BlockSpec((wd_rows, d_out), lambda j, i: (wd_blk(j, i), 0))],
        out_shape=[jax.ShapeDtypeStruct((m, f), BF16), jax.ShapeDtypeStruct((f, d_out), BF16)],
        scratch_shapes=[pltpu.VMEM((2, 2, k, tf), BF16)],
        compiler_params=_params("arbitrary", "arbitrary"),
        name="up_conv_gate",
    )(tiles, w_up, w_up, conv_w, conv_b, w_down)


def _rope_tables(seq, n_batch, n_ctx_rows):
    rows = seq // GRID_W
    r_idx, c_idx = jnp.meshgrid(jnp.arange(rows, dtype=F32), jnp.arange(GRID_W, dtype=F32), indexing='ij')
    axis_dim = HEAD_DIM // 2
    inv_freq = ROPE_BASE ** (-jnp.arange(0, axis_dim, 2, dtype=F32) / axis_dim)
    ang_r = r_idx.reshape(-1)[:, None] * inv_freq
    ang_c = c_idx.reshape(-1)[:, None] * inv_freq
    cr, sr, cc, sn = jnp.cos(ang_r), jnp.sin(ang_r), jnp.cos(ang_c), jnp.sin(ang_c)
    zero = jnp.zeros_like(sr)
    c = jnp.concatenate([cr, cr, cc, cc], axis=-1)
    sa = jnp.concatenate([-sr, zero, -sn, zero], axis=-1)
    sb = jnp.concatenate([zero, sr, zero, sn], axis=-1)

    def full(t, fill):
        return jnp.concatenate([jnp.tile(t, (n_batch, 1)), jnp.full((n_ctx_rows, HEAD_DIM), fill, F32)], axis=0)

    return full(c, 1.0), full(sa, 0.0), full(sb, 0.0)


def kernel(x, c, ctx, c_ctx, w_mod, b_mod, w_in, b_gate, w_branch, w_o, attn_sink, q_norm_g, k_norm_g,
           ret_decay, ret_gn_g, ln1_g, ln1_b, w_up, ffn_conv_w, ffn_conv_b, w_down, ln2_g, ln2_b):
    n_batch, seq, d = x.shape
    ctx_len = ctx.shape[1]
    depth = w_mod.shape[0]
    d_ff = w_down.shape[1]
    lat_rows = n_batch * seq
    m_all = lat_rows + n_batch * ctx_len
    alpha = (2 * depth) ** 0.25
    qkv_w = _N_HEADS_QKV * HEAD_DIM
    assert seq % 256 == 0 and ctx_len % 256 == 0 and lat_rows % ctx_len == 0

    def group_of(row_start):
        return jnp.minimum(row_start // seq, n_batch)

    c_rows = jnp.concatenate([c, c_ctx[None], jnp.zeros((8 - n_batch - 1, d), F32)], axis=0)
    mod = _modulation(c_rows, w_mod, b_mod)
    mod = mod.reshape(depth, 8, 6, d).transpose(0, 2, 1, 3)[:, :, :, None, :]
    SH1, SC1, G1, SH2, SC2, G2 = range(6)

    rope_c, rope_sa, rope_sb = _rope_tables(seq, n_batch, n_batch * ctx_len)
    log_gamma = -jnp.exp(ret_decay.astype(F32))
    b_gate3 = b_gate.reshape(depth, 1, -1)
    gn_g3 = ret_gn_g.reshape(depth, 1, -1)
    ln1_g3, ln1_b3 = ln1_g.reshape(depth, 1, d), ln1_b.reshape(depth, 1, d)
    ln2_g3, ln2_b3 = ln2_g.reshape(depth, 1, d), ln2_b.reshape(depth, 1, d)
    conv_b3 = ffn_conv_b.reshape(depth, 1, d_ff)

    lat = jnp.concatenate([x.reshape(lat_rows, d), ctx.reshape(n_batch * ctx_len, d)], axis=0)
    h = _modulate(lat, mod[0], SC1, SH1, group_of)

    for l in range(depth):
        last = l == depth - 1
        m = lat_rows if last else m_all

        p = _matmul(h, w_in, l, 0, qkv_w, m_all, F32, name="in_proj_qkv")
        gates = _matmul(h, w_in, l, qkv_w, N_BRANCH * d, m, BF16, bias=b_gate3, name="in_proj_gate")
        act, krf, grs = _prep(p, rope_c, rope_sa, rope_sb, q_norm_g[l][None], k_norm_g[l][None])

        tq = 256
        r128 = tq // 128
        lat128 = seq // 128
        ctx_seg = lambda k0, v0: (ctx_len, lambda b, n: lat_rows // ctx_len + b, k0, v0, None)
        segs_a = [
            (128, lambda b, n: b * lat128 + jnp.maximum(n * r128 - 1, 0), _KA, _VA, -128),
            (tq, lambda b, n: b * (seq // tq) + n, _KA, _VA, 0),
            (128, lambda b, n: b * lat128 + jnp.minimum((n + 1) * r128, lat128 - 1), _KA, _VA, tq),
            ctx_seg(_KA, _VA),
        ]
        attn = functools.partial(_attention, act, n_batch=n_batch)
        y_a = attn(segs_a, q_rows0=0, q_len=seq, q_head0=_QA, n_kv=A_KV_HEADS,
                   group=A_HEADS // A_KV_HEADS, tq=tq, sink=attn_sink[l], name="attn_window")
        segs_b = [ctx_seg(_KB, _VB), (seq, lambda b, n: b, _KB, _VB, None)]
        y_b = attn(segs_b, q_rows0=0, q_len=seq, q_head0=_QB, n_kv=B_KV_HEADS,
                   group=B_HEADS // B_KV_HEADS, tq=256, sink=None, name="attn_dense", online=True)
        if not last:
            y_a_c = attn([ctx_seg(_KA, _VA)], q_rows0=lat_rows, q_len=ctx_len, q_head0=_QA,
                         n_kv=A_KV_HEADS, group=A_HEADS // A_KV_HEADS, tq=ctx_len, sink=attn_sink[l],
                         name="attn_ctx_a")
            y_b_c = attn([ctx_seg(_KB, _VB)], q_rows0=lat_rows, q_len=ctx_len, q_head0=_QB,
                         n_kv=B_KV_HEADS, group=B_HEADS // B_KV_HEADS, tq=ctx_len, sink=None,
                         name="attn_ctx_b")
            y_a = jnp.concatenate([y_a, y_a_c], axis=0)
            y_b = jnp.concatenate([y_b, y_b_c], axis=0)

        o_f, o_b = _retention(act, krf, log_gamma[l], n_batch=n_batch, seq=seq, ctx_len=ctx_len)
        y_c = _gn_gate(o_f, o_b, grs, gn_g3, l, m, lat_rows=lat_rows, seq=seq, ctx_len=ctx_len)

        merged = _merge(y_a, y_b, y_c, gates, w_branch, l, m)
        y = _matmul(merged, w_o, l, 0, d, m, F32, name="out_proj")
        lat, h2 = _post_ln(lat, y, mod[l], G1, mod[l], SC2, SH2, ln1_g3, ln1_b3, l, m, group_of, alpha)

        seq_lens = [seq] * n_batch + ([] if last else [ctx_len] * n_batch)
        a, w_down_bf = _up_conv_gate(h2, w_up, ffn_conv_w, conv_b3, w_down, l, m, seq_lens)
        y2 = _matmul_a_resident(a, w_down_bf, m, F32, name="down_proj")
        nxt = None if last else mod[l + 1]
        lat, h = _post_ln(lat, y2, mod[l], G2, nxt, SC1, SH1, ln2_g3, ln2_b3, l, m, group_of, alpha)

    return lat.reshape(n_batch, seq, d)
```

```python
import functools
import math

import jax
import jax.numpy as jnp
from jax import lax
from jax.experimental import pallas as pl
from jax.experimental.pallas import tpu as pltpu

F32 = jnp.float32
BF16 = jnp.bfloat16

GRID_W = 64
HEAD_DIM = 128
A_HEADS = 8
A_KV_HEADS = 2
B_HEADS = 8
B_KV_HEADS = 2
C_HEADS = 8
WINDOW = 128
RET_CHUNK = 256
N_BRANCH = 3
CONV_WIDTH = 3
ROPE_BASE = 10000.0
LN_EPS = 1e-5
RMS_EPS = 1e-6
GN_EPS = 1e-5
NEG_INF = -1e30

V7X_VMEM_BYTES = 64 * 1024 * 1024
VMEM_LIMIT = V7X_VMEM_BYTES - 8 * 1024 * 1024

ATTN_TQ = 256
ATTN_KEYS_ONLINE = 256
ATTN_KEYS_TWO_PASS = 512

_QA, _KA, _VA = 0, 8, 10
_QB, _KB, _VB = 12, 20, 22
_QR, _KR, _VR, _GR = 24, 32, 40, 48
_N_HEADS_QKV = 56
_N_HEADS_ACT = 48


def _params(*sem):
    return pltpu.CompilerParams(dimension_semantics=sem, vmem_limit_bytes=VMEM_LIMIT)


def _tile(n, pref, unit):
    t = min(pref, n) // unit * unit
    while t > unit and n % t:
        t -= unit
    assert t >= unit and n % t == 0, (n, pref, unit)
    return t


def _sigmoid(x):
    return 1.0 / (1.0 + jnp.exp(-x))


def _silu(x):
    return x * _sigmoid(x)


def _mod_kernel(c_ref, w_ref, b_ref, o_ref):
    a = _silu(c_ref[...]).astype(BF16)
    o_ref[...] = jnp.dot(a, w_ref[...].astype(BF16), preferred_element_type=F32) + b_ref[...]


def _modulation(c_rows, w_mod, b_mod):
    depth, d, n = w_mod.shape
    rows = c_rows.shape[0]
    tn = _tile(n, 1024, 128)
    return pl.pallas_call(
        _mod_kernel,
        grid=(depth, n // tn),
        in_specs=[pl.BlockSpec((rows, d), lambda l, j: (0, 0)),
                  pl.BlockSpec((None, d, tn), lambda l, j: (l, 0, j)),
                  pl.BlockSpec((None, 1, tn), lambda l, j: (l, 0, j))],
        out_specs=pl.BlockSpec((None, rows, tn), lambda l, j: (l, 0, j)),
        out_shape=jax.ShapeDtypeStruct((depth, rows, n), F32),
        compiler_params=_params("parallel", "parallel"),
        name="modulation",
    )(c_rows, w_mod, b_mod.reshape(depth, 1, n))


def _modulate_kernel(x_ref, sc_ref, sh_ref, o_ref):
    o_ref[...] = (x_ref[...] * (1.0 + sc_ref[...]) + sh_ref[...]).astype(o_ref.dtype)


def _modulate(x, mod, k_scale, k_shift, group_of):
    m, d = x.shape
    tm = 256
    vec = lambda k: pl.BlockSpec((None, None, 1, d), lambda i: (k, group_of(i * tm), 0, 0))
    return pl.pallas_call(
        _modulate_kernel,
        grid=(m // tm,),
        in_specs=[pl.BlockSpec((tm, d), lambda i: (i, 0)), vec(k_scale), vec(k_shift)],
        out_specs=pl.BlockSpec((tm, d), lambda i: (i, 0)),
        out_shape=jax.ShapeDtypeStruct((m, d), BF16),
        compiler_params=_params("parallel"),
        name="modulate",
    )(x, mod, mod)


def _staged_phases(j, stage, compute):
    @pl.when(j == 0)
    def _():
        stage(0)

    for parity in (0, 1):
        @pl.when((j > 0) & (j % 2 == parity))
        def _(parity=parity):
            compute(1 - parity)
            stage(parity)


def _mm_kernel(a_ref, w_ref, *rest, kc, tm, has_bias, resid):
    b_ref = rest[0] if has_bias else None
    o_ref, wbs = rest[-3], rest[-2:]
    j, i = pl.program_id(0), pl.program_id(1)

    def stage(s):
        wbs[s][pl.ds(pl.multiple_of(i * kc, kc), kc), :] = w_ref[...].astype(BF16)

    def compute(s):
        acc = jnp.dot(a_ref[...], wbs[s][...], preferred_element_type=F32)
        if has_bias:
            acc = _sigmoid(acc + b_ref[...])
        if resid is not None:
            alpha, bounds = resid
            lat_ref, gates_ref = rest[-5], rest[-4]
            row = i * tm + lax.broadcasted_iota(jnp.int32, (tm, 1), 0)
            gate = gates_ref[len(bounds):len(bounds) + 1, :]
            for g in reversed(range(len(bounds))):
                gate = jnp.where(row < bounds[g], gates_ref[g:g + 1, :], gate)
            acc = alpha * lat_ref[...] + gate * acc
        o_ref[...] = acc.astype(o_ref.dtype)

    _staged_phases(j, stage, compute)


def _mm_resid_kernel(a_ref, w_ref, lat_ref, gate_ref, o_ref, *, alpha):
    acc = jnp.dot(a_ref[...], w_ref[...], preferred_element_type=F32)
    o_ref[...] = alpha * lat_ref[...] + gate_ref[...] * acc


def _matmul_a_resident(a, w, m, lat, mod, k_gate, group_of, alpha, *, tm=512, tn=512, name="matmul"):
    k = a.shape[1]
    n = w.shape[-1]
    tm = _tile(m, tm, 8)
    tn = _tile(n, tn, 128)
    return pl.pallas_call(
        functools.partial(_mm_resid_kernel, alpha=alpha),
        grid=(m // tm, n // tn),
        in_specs=[pl.BlockSpec((tm, k), lambda i, j: (i, 0)),
                  pl.BlockSpec((k, tn), lambda i, j: (0, j)),
                  pl.BlockSpec((tm, tn), lambda i, j: (i, j)),
                  pl.BlockSpec((None, None, 1, tn), lambda i, j: (k_gate, group_of(i * tm), 0, j))],
        out_specs=pl.BlockSpec((tm, tn), lambda i, j: (i, j)),
        out_shape=jax.ShapeDtypeStruct((m, n), F32),
        compiler_params=_params("parallel", "arbitrary"),
        name=name,
    )(a, w, lat, mod)


_MM_ROW_TILES = 8


def _matmul(a, w, layer, n0, n, m, out_dtype, *, tn=1024, bias=None, resid=None, name="matmul"):
    k = a.shape[1]
    ni = _MM_ROW_TILES
    tm, kc = m // ni, k // ni
    assert m % ni == 0 and tm % 8 == 0 and k % ni == 0 and kc % 16 == 0
    tn = _tile(math.gcd(n, n0) if n0 else n, tn, 128)
    j0, nj = n0 // tn, n // tn
    row = lambda j, i: jnp.where(j == 0, 0, i)
    col = lambda j: jnp.maximum(j - 1, 0)
    in_specs = [pl.BlockSpec((tm, k), lambda j, i: (row(j, i), 0)),
                pl.BlockSpec((None, kc, tn),
                             lambda j, i: (layer, jnp.where(j == nj, ni - 1, i), jnp.minimum(j, nj - 1) + j0))]
    args = [a, w]
    if bias is not None:
        in_specs.append(pl.BlockSpec((None, 1, tn), lambda j, i: (layer, 0, col(j))))
        args.append(bias)
    static_resid = None
    if resid is not None:
        lat, gates, alpha, bounds = resid
        in_specs += [pl.BlockSpec((tm, tn), lambda j, i: (row(j, i), col(j))),
                     pl.BlockSpec((gates.shape[0], tn), lambda j, i: (0, col(j)))]
        args += [lat, gates]
        static_resid = (alpha, tuple(bounds))
    return pl.pallas_call(
        functools.partial(_mm_kernel, kc=kc, tm=tm, has_bias=bias is not None, resid=static_resid),
        grid=(nj + 1, ni),
        in_specs=in_specs,
        out_specs=pl.BlockSpec((tm, tn), lambda j, i: (row(j, i), col(j))),
        out_shape=jax.ShapeDtypeStruct((m, n), out_dtype),
        scratch_shapes=[pltpu.VMEM((k, tn), BF16)] * 2,
        compiler_params=_params("arbitrary", "arbitrary"),
        name=name,
    )(*args)


def _rope(x, c, sa, sb):
    return x * c + pltpu.roll(x, 96, 1) * sa + pltpu.roll(x, 32, 1) * sb


def _rms(x, g):
    return x * lax.rsqrt(jnp.mean(x * x, axis=-1, keepdims=True) + RMS_EPS) * g


def _prep_kernel(p_ref, c_ref, sa_ref, sb_ref, qg_ref, kg_ref, act_ref, krf_ref, grs_ref):
    c, sa, sb = c_ref[...], sa_ref[...], sb_ref[...]
    qg, kg = qg_ref[...], kg_ref[...]
    k_scale = HEAD_DIM ** -0.5
    for h in range(_N_HEADS_QKV):
        cols = slice(h * HEAD_DIM, (h + 1) * HEAD_DIM)
        x = p_ref[:, cols]
        if h < _KA + A_KV_HEADS:
            y = _rope(x, c, sa, sb)
        elif h < _QB:
            y = x
        elif h < _KB:
            y = _rope(_rms(x, qg), c, sa, sb)
        elif h < _VB:
            y = _rope(_rms(x, kg), c, sa, sb)
        elif h < _QR:
            y = x
        elif h < _KR:
            y = _rope(x, c, sa, sb)
        elif h < _VR:
            y = _rope(x, c, sa, sb) * k_scale
            krf_ref[:, (h - _KR) * HEAD_DIM:(h - _KR + 1) * HEAD_DIM] = y
        elif h < _GR:
            y = x
        else:
            grs_ref[:, (h - _GR) * HEAD_DIM:(h - _GR + 1) * HEAD_DIM] = _silu(x)
            continue
        act_ref[:, cols] = y.astype(BF16)


def _prep(p, rope_c, rope_sa, rope_sb, qg, kg):
    m = p.shape[0]
    tm = 256
    row = lambda w: pl.BlockSpec((tm, w), lambda i: (i, 0))
    vec = pl.BlockSpec((1, HEAD_DIM), lambda i: (0, 0))
    return pl.pallas_call(
        _prep_kernel,
        grid=(m // tm,),
        in_specs=[row(_N_HEADS_QKV * HEAD_DIM), row(HEAD_DIM), row(HEAD_DIM), row(HEAD_DIM), vec, vec],
        out_specs=[row(_N_HEADS_ACT * HEAD_DIM), row(C_HEADS * HEAD_DIM), row(C_HEADS * HEAD_DIM)],
        out_shape=[jax.ShapeDtypeStruct((m, _N_HEADS_ACT * HEAD_DIM), BF16),
                   jax.ShapeDtypeStruct((m, C_HEADS * HEAD_DIM), F32),
                   jax.ShapeDtypeStruct((m, C_HEADS * HEAD_DIM), F32)],
        compiler_params=_params("parallel"),
        name="prep",
    )(p, rope_c, rope_sa, rope_sb, qg, kg)


def _attn_head(q_ref, kv, sink_vals, o_ref, o_col0, *, n, bands, tq, seq, group, kchunk, online):
    n_seg = len(bands)
    scale = HEAD_DIM ** -0.5
    log2e = math.log2(math.e)
    q = jnp.concatenate([q_ref[:, g * HEAD_DIM:(g + 1) * HEAD_DIM] for g in range(group)], axis=0)

    def raw_scores(s, c0, c1):
        sc = lax.dot_general(q, kv[2 * s][c0:c1, :], (((1,), (1,)), ((), ())), preferred_element_type=F32)
        if bands[s] is not None:
            qpos = n * tq + lax.broadcasted_iota(jnp.int32, (tq, c1 - c0), 0)
            kpos = n * tq + (bands[s] + c0) + lax.broadcasted_iota(jnp.int32, (tq, c1 - c0), 1)
            valid = (jnp.abs(qpos - kpos) <= WINDOW) & (kpos >= 0) & (kpos < seq)
            sc = jnp.where(jnp.concatenate([valid] * group, axis=0), sc, NEG_INF)
        return sc

    chunks = [(s, c0, min(c0 + kchunk, kv[2 * s].shape[0]))
              for s in range(n_seg) for c0 in range(0, kv[2 * s].shape[0], kchunk)]
    if online:
        c2 = scale * log2e
        sink2 = None
        if sink_vals is not None:
            sink2 = jnp.concatenate([jnp.full((tq, 1), v, F32) for v in sink_vals], axis=0) * log2e
        m2, den, out = sink2, None, None
        for s, c0, c1 in chunks:
            sc = raw_scores(s, c0, c1)
            cm = jnp.max(sc, axis=-1, keepdims=True) * c2
            m_new = cm if m2 is None else jnp.maximum(m2, cm)
            e = jnp.exp2(sc * c2 - m_new)
            r = jnp.sum(e, axis=-1, keepdims=True)
            o = jnp.dot(e.astype(BF16), kv[2 * s + 1][c0:c1, :], preferred_element_type=F32)
            if den is None:
                den, out = r, o
            else:
                alpha = jnp.exp2(m2 - m_new)
                den, out = den * alpha + r, out * alpha + o
            m2 = m_new
        if sink2 is not None:
            den = den + jnp.exp2(sink2 - m2)
        out = out / den
        for g in range(group):
            c = o_col0 + g * HEAD_DIM
            o_ref[:, c:c + HEAD_DIM] = out[g * tq:(g + 1) * tq].astype(o_ref.dtype)
        return

    scores = [raw_scores(*ch) for ch in chunks]
    mx = functools.reduce(jnp.maximum, [jnp.max(sc, axis=-1, keepdims=True) for sc in scores])
    mx = mx * scale
    if sink_vals is not None:
        sink = jnp.concatenate([jnp.full((tq, 1), v, F32) for v in sink_vals], axis=0)
        mx = jnp.maximum(mx, sink)
    mx2 = mx * log2e
    den, out = None, None
    for (s, c0, c1), sc in zip(chunks, scores):
        e = jnp.exp2(sc * (scale * log2e) - mx2)
        r = jnp.sum(e, axis=-1, keepdims=True)
        o = jnp.dot(e.astype(BF16), kv[2 * s + 1][c0:c1, :], preferred_element_type=F32)
        den = r if den is None else den + r
        out = o if out is None else out + o
    if sink_vals is not None:
        den = den + jnp.exp2(sink * log2e - mx2)
    out = out / den
    for g in range(group):
        c = o_col0 + g * HEAD_DIM
        o_ref[:, c:c + HEAD_DIM] = out[g * tq:(g + 1) * tq].astype(o_ref.dtype)


def _attn_kernel(*refs, n_kv, n_seg, bands, has_sink, tq, seq, group, kchunk, online):
    n = pl.program_id(1)
    sink_ref = refs[-2] if has_sink else None
    o_ref = refs[-1]
    for h in range(n_kv):
        kv = refs[n_kv + h * 2 * n_seg:n_kv + (h + 1) * 2 * n_seg]
        sink_vals = [sink_ref[h * group + g] for g in range(group)] if has_sink else None
        _attn_head(refs[h], kv, sink_vals, o_ref, h * group * HEAD_DIM,
                   n=n, bands=bands, tq=tq, seq=seq, group=group, kchunk=kchunk, online=online)


def _attention(act, segs, *, n_batch, q_rows0, q_len, q_head0, n_kv, group, tq, sink, name, online=False):
    gw = group * HEAD_DIM
    nq = q_len // tq
    assert q_rows0 % tq == 0 and q_len % tq == 0 and q_head0 % group == 0
    in_specs, args = [], []
    for h in range(n_kv):
        in_specs.append(pl.BlockSpec(
            (tq, gw), lambda b, n, h=h: (q_rows0 // tq + b * nq + n, q_head0 // group + h)))
        args.append(act)
    for h in range(n_kv):
        for rows, row_fn, k0, v0, _ in segs:
            for c0 in (k0, v0):
                in_specs.append(pl.BlockSpec(
                    (rows, HEAD_DIM), lambda b, n, row_fn=row_fn, c=c0 + h: (row_fn(b, n), c)))
                args.append(act)
    if sink is not None:
        in_specs.append(pl.BlockSpec(memory_space=pltpu.SMEM))
        args.append(sink)
    body = functools.partial(
        _attn_kernel, n_kv=n_kv, n_seg=len(segs), bands=tuple(s[4] for s in segs),
        has_sink=sink is not None, tq=tq, seq=q_len, group=group,
        kchunk=ATTN_KEYS_ONLINE if online else ATTN_KEYS_TWO_PASS, online=online)
    return pl.pallas_call(
        body,
        grid=(n_batch, nq),
        in_specs=in_specs,
        out_specs=pl.BlockSpec((tq, n_kv * gw), lambda b, n: (b * nq + n, 0)),
        out_shape=jax.ShapeDtypeStruct((n_batch * q_len, n_kv * gw), BF16),
        compiler_params=_params("parallel", "arbitrary"),
        name=name,
    )(*args)


def _retention_kernel(lg_ref, *refs, n_batch):
    c = RET_CHUNK
    of_ref, ob_ref, state_ref = refs[-3:]

    @pl.when(pl.program_id(0) == 0)
    def _():
        state_ref[...] = jnp.zeros_like(state_ref)

    row = lax.broadcasted_iota(jnp.int32, (c, c), 0)
    col = lax.broadcasted_iota(jnp.int32, (c, c), 1)
    pos = lax.broadcasted_iota(jnp.int32, (c, 1), 0).astype(F32)
    for d, o_ref in enumerate((of_ref, ob_ref)):
        diff = (row - col) if d == 0 else (col - row)
        keep = (diff >= 0) if d == 0 else (diff > 0)
        dist = jnp.maximum(diff, 0).astype(F32)
        q_pow = (pos + 1.0) if d == 0 else (c - pos)
        k_pow = (c - 1.0 - pos) if d == 0 else pos
        for h in range(C_HEADS):
            cols = slice(h * HEAD_DIM, (h + 1) * HEAD_DIM)
            lg = lg_ref[d, h]
            d_intra = jnp.where(keep, jnp.exp(lg * dist), 0.0)
            d_q = jnp.exp(lg * q_pow)
            d_k = jnp.exp(lg * k_pow)
            d_c = jnp.exp(jnp.full((1, 1), lg * c, F32))
            for b in range(n_batch):
                q_ref, k_ref, kfull_ref, v_ref = refs[(2 * b + d) * 4:(2 * b + d) * 4 + 4]
                q, k, v = q_ref[:, cols], k_ref[:, cols], v_ref[:, cols]
                state = state_ref[b, d, h]
                att = lax.dot_general(q, k, (((1,), (1,)), ((), ())), preferred_element_type=F32) * d_intra
                o = (jnp.dot(att.astype(BF16), v, preferred_element_type=F32)
                     + jnp.dot(q, state.astype(BF16), preferred_element_type=F32) * d_q)
                o_ref[b, :, cols] = o
                kd = (kfull_ref[:, cols] * d_k).T.astype(BF16)
                state_ref[b, d, h] = state * d_c + jnp.dot(kd, v, preferred_element_type=F32)


def _retention(act, krf, log_gamma, *, n_batch, seq, ctx_len):
    c = RET_CHUNK
    w = C_HEADS * HEAD_DIM
    n_ctx, n_lat = ctx_len // c, seq // c
    n_steps = n_ctx + n_lat
    ctx0 = n_batch * seq // c

    def fwd_pos(t):
        return t

    def bwd_pos(t):
        return jnp.where(t < n_ctx, n_ctx - 1 - t, n_steps - 1 - (t - n_ctx))

    def global_chunk(b, p):
        return jnp.where(p < n_ctx, ctx0 + b * n_ctx + p, b * n_lat + p - n_ctx)

    def spec(b, pos_fn, col_block):
        return pl.BlockSpec((c, w), lambda t: (global_chunk(b, pos_fn(t)), col_block))

    qc, kc, vc = _QR * HEAD_DIM // w, _KR * HEAD_DIM // w, _VR * HEAD_DIM // w
    in_specs, args = [pl.BlockSpec(memory_space=pltpu.SMEM)], [log_gamma]
    for b in range(n_batch):
        for pos_fn in (fwd_pos, bwd_pos):
            in_specs += [spec(b, pos_fn, qc), spec(b, pos_fn, kc), spec(b, pos_fn, 0), spec(b, pos_fn, vc)]
            args += [act, act, krf, act]
    out_shape = jax.ShapeDtypeStruct((n_batch, ctx_len + seq, w), F32)
    return pl.pallas_call(
        functools.partial(_retention_kernel, n_batch=n_batch),
        grid=(n_steps,),
        in_specs=in_specs,
        out_specs=[pl.BlockSpec((n_batch, c, w), lambda t: (0, fwd_pos(t), 0)),
                   pl.BlockSpec((n_batch, c, w), lambda t: (0, bwd_pos(t), 0))],
        out_shape=[out_shape, out_shape],
        scratch_shapes=[pltpu.VMEM((n_batch, 2, C_HEADS, HEAD_DIM, HEAD_DIM), F32)],
        compiler_params=_params("arbitrary"),
        name="retention",
    )(*args)


def _gn_gate_kernel(of_ref, ob_ref, grs_ref, g_ref, o_ref):
    for h in range(C_HEADS):
        cols = slice(h * HEAD_DIM, (h + 1) * HEAD_DIM)
        o = of_ref[:, cols] + ob_ref[:, cols]
        mu = jnp.mean(o, axis=-1, keepdims=True)
        var = jnp.mean(jnp.square(o - mu), axis=-1, keepdims=True)
        y = (o - mu) * lax.rsqrt(var + GN_EPS) * g_ref[:, cols]
        o_ref[:, cols] = (grs_ref[:, cols] * y).astype(o_ref.dtype)


def _gn_gate(o_f, o_b, grs, gn_g, layer, m, *, lat_rows, seq, ctx_len):
    w = o_f.shape[-1]
    tm = _tile(ctx_len, 256, 8)

    def scan_block(i):
        r = i * tm
        in_lat = r < lat_rows
        b = jnp.where(in_lat, r // seq, (r - lat_rows) // ctx_len)
        off = jnp.where(in_lat, ctx_len + r % seq, (r - lat_rows) % ctx_len)
        return b, off // tm, 0

    row = pl.BlockSpec((tm, w), lambda i: (i, 0))
    scan = pl.BlockSpec((None, tm, w), scan_block)
    return pl.pallas_call(
        _gn_gate_kernel,
        grid=(m // tm,),
        in_specs=[scan, scan, row, pl.BlockSpec((None, 1, w), lambda i: (layer, 0, 0))],
        out_specs=row,
        out_shape=jax.ShapeDtypeStruct((m, w), BF16),
        compiler_params=_params("parallel"),
        name="gn_gate",
    )(o_f, o_b, grs, gn_g)


def _merge_kernel(ya_ref, yb_ref, yc_ref, w_ref, ga_ref, gb_ref, gc_ref, o_ref, *wbs, kc):
    j, i = pl.program_id(0), pl.program_id(1)

    def stage(s):
        wbs[s][:, pl.ds(pl.multiple_of(i * kc, kc), kc), :] = w_ref[...].astype(BF16)

    def compute(s):
        out = None
        for br, (y_ref, g_ref) in enumerate(((ya_ref, ga_ref), (yb_ref, gb_ref), (yc_ref, gc_ref))):
            t = g_ref[...].astype(F32) * jnp.dot(y_ref[...], wbs[s][br], preferred_element_type=F32)
            out = t if out is None else out + t
        o_ref[...] = out.astype(o_ref.dtype)

    _staged_phases(j, stage, compute)


def _merge(y_a, y_b, y_c, gates, w_branch, layer, m):
    d = w_branch.shape[-1]
    bw = w_branch.shape[-2]
    ni = _MM_ROW_TILES
    tm, kc = m // ni, bw // ni
    assert m % ni == 0 and tm % 8 == 0 and bw % ni == 0 and kc % 16 == 0
    tn = _tile(d, 1024, 128)
    nj = d // tn
    row = lambda j, i: jnp.where(j == 0, 0, i)
    col = lambda j: jnp.maximum(j - 1, 0)
    y_spec = pl.BlockSpec((tm, bw), lambda j, i: (row(j, i), 0))
    g_spec = lambda br: pl.BlockSpec((tm, tn), lambda j, i: (row(j, i), br * nj + col(j)))
    return pl.pallas_call(
        functools.partial(_merge_kernel, kc=kc),
        grid=(nj + 1, ni),
        in_specs=[y_spec, y_spec, y_spec,
                  pl.BlockSpec((None, N_BRANCH, kc, tn),
                               lambda j, i: (layer, 0, jnp.where(j == nj, ni - 1, i), jnp.minimum(j, nj - 1))),
                  g_spec(0), g_spec(1), g_spec(2)],
        out_specs=pl.BlockSpec((tm, tn), lambda j, i: (row(j, i), col(j))),
        out_shape=jax.ShapeDtypeStruct((m, d), BF16),
        scratch_shapes=[pltpu.VMEM((N_BRANCH, bw, tn), BF16)] * 2,
        compiler_params=_params("arbitrary", "arbitrary"),
        name="merge",
    )(y_a, y_b, y_c, w_branch, gates, gates, gates)


def _ln_kernel(z_ref, lng_ref, lnb_ref, *rest, emit_h):
    z = z_ref[...]
    mu = jnp.mean(z, axis=-1, keepdims=True)
    var = jnp.mean(jnp.square(z - mu), axis=-1, keepdims=True)
    out = (z - mu) * lax.rsqrt(var + LN_EPS) * lng_ref[...] + lnb_ref[...]
    if emit_h:
        sc_ref, sh_ref, lat_out, h_out = rest
        h_out[...] = (out * (1.0 + sc_ref[...]) + sh_ref[...]).astype(h_out.dtype)
    else:
        lat_out, = rest
    lat_out[...] = out


def _post_ln(z, mod_next, k_scale, k_shift, ln_g, ln_b, layer, m, group_of):
    d = z.shape[1]
    tm = 256
    emit_h = mod_next is not None
    row = pl.BlockSpec((tm, d), lambda i: (i, 0))
    vec = lambda k: pl.BlockSpec((None, None, 1, d), lambda i: (k, group_of(i * tm), 0, 0))
    par = pl.BlockSpec((None, 1, d), lambda i: (layer, 0, 0))
    in_specs = [row, par, par]
    args = [z, ln_g, ln_b]
    out_specs = [row]
    out_shape = [jax.ShapeDtypeStruct((m, d), F32)]
    if emit_h:
        in_specs += [vec(k_scale), vec(k_shift)]
        args += [mod_next, mod_next]
        out_specs.append(row)
        out_shape.append(jax.ShapeDtypeStruct((m, d), BF16))
    res = pl.pallas_call(
        functools.partial(_ln_kernel, emit_h=emit_h),
        grid=(m // tm,),
        in_specs=in_specs,
        out_specs=out_specs,
        out_shape=out_shape,
        compiler_params=_params("parallel"),
        name="post_ln",
    )(*args)
    return (res[0], res[1]) if emit_h else (res[0], None)


_HALO = 8
_UP_ROW_TILES = 4


def _up_conv_kernel(a_ref, wg_ref, wu_ref, cw_ref, cb_ref, wd_ref, o_ref, wd_out_ref, *wbs,
                    kc, tm, n_sub, seq_first_rows, seq_last_rows):
    j, i = pl.program_id(0), pl.program_id(1)

    def stage(s):
        wd_out_ref[...] = wd_ref[...].astype(BF16)
        rows = pl.ds(pl.multiple_of(i * kc, kc), kc)
        wbs[s][0, rows, :] = wg_ref[...].astype(BF16)
        wbs[s][1, rows, :] = wu_ref[...].astype(BF16)

    def compute(s):
        ts = tm // n_sub
        n_rows = ts + 2 * _HALO
        for k in range(n_sub):
            r0 = k * ts
            g = jnp.dot(a_ref[r0:r0 + n_rows, :], wbs[s][0], preferred_element_type=F32)
            u = jnp.dot(a_ref[r0 + _HALO:r0 + _HALO + ts, :], wbs[s][1], preferred_element_type=F32)
            g_prev = pltpu.roll(g, 1, 0)[_HALO:_HALO + ts]
            g_next = pltpu.roll(g, n_rows - 1, 0)[_HALO:_HALO + ts]
            row = i * tm + r0 + lax.broadcasted_iota(jnp.int32, (ts, 1), 0)
            has_prev = functools.reduce(lambda x, y: x & y, [row != r for r in seq_first_rows])
            has_next = functools.reduce(lambda x, y: x & y, [row != r for r in seq_last_rows])
            conv = (cb_ref[...] + jnp.where(has_prev, g_prev, 0.0) * cw_ref[0:1, :]
                    + g[_HALO:_HALO + ts] * cw_ref[1:2, :] + jnp.where(has_next, g_next, 0.0) * cw_ref[2:3, :])
            o_ref[r0:r0 + ts, :] = (_silu(conv) * u).astype(o_ref.dtype)

    _staged_phases(j, stage, compute)


def _up_conv_gate(h, w_up, conv_w, conv_b, w_down, layer, m, seq_lens):
    k = h.shape[1]
    f = w_up.shape[-1] // 2
    d_out = w_down.shape[-1]
    ni = _UP_ROW_TILES
    tm, kc = m // ni, k // ni
    assert m % ni == 0 and tm % 8 == 0 and k % ni == 0 and kc % 16 == 0 and sum(seq_lens) == m
    tf = _tile(f, 256, 128)
    nj = f // tf
    wd_rows = f // (nj * ni)
    assert f % (nj * ni) == 0 and wd_rows % 16 == 0
    wd_blk = lambda j, i: jnp.maximum(j - 1, 0) * ni + jnp.where(j == 0, 0, i)
    starts = [sum(seq_lens[:s]) for s in range(len(seq_lens))]
    first_rows = tuple(starts)
    last_rows = tuple(st + n - 1 for st, n in zip(starts, seq_lens))
    edge = jnp.zeros((_HALO, k), h.dtype)
    pieces = []
    for t in range(ni):
        lo, hi = t * tm, (t + 1) * tm
        pieces += [h[lo - _HALO:lo] if t else edge, h[lo:hi], h[hi:hi + _HALO] if hi < m else edge]
    tiles = jnp.concatenate(pieces, axis=0).reshape(ni, tm + 2 * _HALO, k)
    row = lambda j, i: jnp.where(j == 0, 0, i)
    col = lambda j: jnp.maximum(j - 1, 0)
    chunk = lambda j, i: jnp.where(j == nj, ni - 1, i)
    stage = lambda j: jnp.minimum(j, nj - 1)
    n_sub = max(1, tm // 512)
    assert tm % (16 * n_sub) == 0
    body = functools.partial(_up_conv_kernel, kc=kc, tm=tm, n_sub=n_sub,
                             seq_first_rows=first_rows, seq_last_rows=last_rows)
    return pl.pallas_call(
        body,
        grid=(nj + 1, ni),
        in_specs=[pl.BlockSpec((None, tm + 2 * _HALO, k), lambda j, i: (row(j, i), 0, 0)),
                  pl.BlockSpec((None, kc, tf), lambda j, i: (layer, chunk(j, i), stage(j))),
                  pl.BlockSpec((None, kc, tf), lambda j, i: (layer, chunk(j, i), nj + stage(j))),
                  pl.BlockSpec((None, CONV_WIDTH, tf), lambda j, i: (layer, 0, col(j))),
                  pl.BlockSpec((None, 1, tf), lambda j, i: (layer, 0, col(j))),
                  pl.BlockSpec((None, wd_rows, d_out), lambda j, i: (layer, wd_blk(j, i), 0))],
        out_specs=[pl.BlockSpec((tm, tf), lambda j, i: (row(j, i), col(j))),
                   pl.BlockSpec((wd_rows, d_out), lambda j, i: (wd_blk(j, i), 0))],
        out_shape=[jax.ShapeDtypeStruct((m, f), BF16), jax.ShapeDtypeStruct((f, d_out), BF16)],
        scratch_shapes=[pltpu.VMEM((2, k, tf), BF16)] * 2,
        compiler_params=_params("arbitrary", "arbitrary"),
        name="up_conv_gate",
    )(tiles, w_up, w_up, conv_w, conv_b, w_down)


def _rope_tables(seq, n_batch, n_ctx_rows):
    rows = seq // GRID_W
    r_idx, c_idx = jnp.meshgrid(jnp.arange(rows, dtype=F32), jnp.arange(GRID_W, dtype=F32), indexing='ij')
    axis_dim = HEAD_DIM // 2
    inv_freq = ROPE_BASE ** (-jnp.arange(0, axis_dim, 2, dtype=F32) / axis_dim)
    ang_r = r_idx.reshape(-1)[:, None] * inv_freq
    ang_c = c_idx.reshape(-1)[:, None] * inv_freq
    cr, sr, cc, sn = jnp.cos(ang_r), jnp.sin(ang_r), jnp.cos(ang_c), jnp.sin(ang_c)
    zero = jnp.zeros_like(sr)
    c = jnp.concatenate([cr, cr, cc, cc], axis=-1)
    sa = jnp.concatenate([-sr, zero, -sn, zero], axis=-1)
    sb = jnp.concatenate([zero, sr, zero, sn], axis=-1)

    def full(t, fill):
        return jnp.concatenate([jnp.tile(t, (n_batch, 1)), jnp.full((n_ctx_rows, HEAD_DIM), fill, F32)], axis=0)

    return full(c, 1.0), full(sa, 0.0), full(sb, 0.0)


def kernel(x, c, ctx, c_ctx, w_mod, b_mod, w_in, b_gate, w_branch, w_o, attn_sink, q_norm_g, k_norm_g,
           ret_decay, ret_gn_g, ln1_g, ln1_b, w_up, ffn_conv_w, ffn_conv_b, w_down, ln2_g, ln2_b):
    n_batch, seq, d = x.shape
    ctx_len = ctx.shape[1]
    depth = w_mod.shape[0]
    d_ff = w_down.shape[1]
    lat_rows = n_batch * seq
    m_all = lat_rows + n_batch * ctx_len
    alpha = (2 * depth) ** 0.25
    qkv_w = _N_HEADS_QKV * HEAD_DIM
    assert seq % 256 == 0 and ctx_len % 256 == 0 and lat_rows % ctx_len == 0

    def group_of(row_start):
        return jnp.minimum(row_start // seq, n_batch)

    group_bounds = [seq * (b + 1) for b in range(n_batch)]

    c_rows = jnp.concatenate([c, c_ctx[None], jnp.zeros((8 - n_batch - 1, d), F32)], axis=0)
    mod = _modulation(c_rows, w_mod, b_mod)
    mod = mod.reshape(depth, 8, 6, d).transpose(0, 2, 1, 3)[:, :, :, None, :]
    SH1, SC1, G1, SH2, SC2, G2 = range(6)

    rope_c, rope_sa, rope_sb = _rope_tables(seq, n_batch, n_batch * ctx_len)
    log_gamma = -jnp.exp(ret_decay.astype(F32))
    b_gate3 = b_gate.reshape(depth, 1, -1)
    gn_g3 = ret_gn_g.reshape(depth, 1, -1)
    ln1_g3, ln1_b3 = ln1_g.reshape(depth, 1, d), ln1_b.reshape(depth, 1, d)
    ln2_g3, ln2_b3 = ln2_g.reshape(depth, 1, d), ln2_b.reshape(depth, 1, d)
    conv_b3 = ffn_conv_b.reshape(depth, 1, d_ff)

    lat = jnp.concatenate([x.reshape(lat_rows, d), ctx.reshape(n_batch * ctx_len, d)], axis=0)
    h = _modulate(lat, mod[0], SC1, SH1, group_of)

    for l in range(depth):
        last = l == depth - 1
        m = lat_rows if last else m_all

        p = _matmul(h, w_in, l, 0, qkv_w, m_all, F32, name="in_proj_qkv")
        gates = _matmul(h, w_in, l, qkv_w, N_BRANCH * d, m, BF16, bias=b_gate3, name="in_proj_gate")
        act, krf, grs = _prep(p, rope_c, rope_sa, rope_sb, q_norm_g[l][None], k_norm_g[l][None])

        tq = ATTN_TQ
        r128 = tq // 128
        lat128 = seq // 128
        ctx_seg = lambda k0, v0: (ctx_len, lambda b, n: lat_rows // ctx_len + b, k0, v0, None)
        segs_a = [
            (128, lambda b, n: b * lat128 + jnp.maximum(n * r128 - 1, 0), _KA, _VA, -128),
            (tq, lambda b, n: b * (seq // tq) + n, _KA, _VA, 0),
            (128, lambda b, n: b * lat128 + jnp.minimum((n + 1) * r128, lat128 - 1), _KA, _VA, tq),
            ctx_seg(_KA, _VA),
        ]
        attn = functools.partial(_attention, act, n_batch=n_batch)
        y_a = attn(segs_a, q_rows0=0, q_len=seq, q_head0=_QA, n_kv=A_KV_HEADS,
                   group=A_HEADS // A_KV_HEADS, tq=tq, sink=attn_sink[l], name="attn_window")
        segs_b = [ctx_seg(_KB, _VB), (seq, lambda b, n: b, _KB, _VB, None)]
        y_b = attn(segs_b, q_rows0=0, q_len=seq, q_head0=_QB, n_kv=B_KV_HEADS,
                   group=B_HEADS // B_KV_HEADS, tq=ATTN_TQ, sink=None, name="attn_dense", online=True)
        if not last:
            y_a_c = attn([ctx_seg(_KA, _VA)], q_rows0=lat_rows, q_len=ctx_len, q_head0=_QA,
                         n_kv=A_KV_HEADS, group=A_HEADS // A_KV_HEADS, tq=ctx_len, sink=attn_sink[l],
                         name="attn_ctx_a")
            y_b_c = attn([ctx_seg(_KB, _VB)], q_rows0=lat_rows, q_len=ctx_len, q_head0=_QB,
                         n_kv=B_KV_HEADS, group=B_HEADS // B_KV_HEADS, tq=ctx_len, sink=None,
                         name="attn_ctx_b")
            y_a = jnp.concatenate([y_a, y_a_c], axis=0)
            y_b = jnp.concatenate([y_b, y_b_c], axis=0)

        o_f, o_b = _retention(act, krf, log_gamma[l], n_batch=n_batch, seq=seq, ctx_len=ctx_len)
        y_c = _gn_gate(o_f, o_b, grs, gn_g3, l, m, lat_rows=lat_rows, seq=seq, ctx_len=ctx_len)

        merged = _merge(y_a, y_b, y_c, gates, w_branch, l, m)
        g1_rows = mod[l, G1].reshape(8, d)
        z = _matmul(merged, w_o, l, 0, d, m, F32, tn=512, name="out_proj",
                    resid=(lat, g1_rows, alpha, group_bounds))
        lat, h2 = _post_ln(z, mod[l], SC2, SH2, ln1_g3, ln1_b3, l, m, group_of)

        seq_lens = [seq] * n_batch + ([] if last else [ctx_len] * n_batch)
        a, w_down_bf = _up_conv_gate(h2, w_up, ffn_conv_w, conv_b3, w_down, l, m, seq_lens)
        z2 = _matmul_a_resident(a, w_down_bf, m, lat, mod[l], G2, group_of, alpha, name="down_proj")
        nxt = None if last else mod[l + 1]
        lat, h = _post_ln(z2, nxt, SC1, SH1, ln2_g3, ln2_b3, l, m, group_of)

    return lat.reshape(n_batch, seq, d)
```

```python
import functools
import math

import jax
import jax.numpy as jnp
from jax import lax
from jax.experimental import pallas as pl
from jax.experimental.pallas import tpu as pltpu

F32 = jnp.float32
BF16 = jnp.bfloat16

GRID_W = 64
HEAD_DIM = 128
A_HEADS = 8
A_KV_HEADS = 2
B_HEADS = 8
B_KV_HEADS = 2
C_HEADS = 8
WINDOW = 128
RET_CHUNK = 256
N_BRANCH = 3
CONV_WIDTH = 3
ROPE_BASE = 10000.0
LN_EPS = 1e-5
RMS_EPS = 1e-6
GN_EPS = 1e-5
NEG_INF = -1e30

V7X_VMEM_BYTES = 64 * 1024 * 1024
VMEM_LIMIT = V7X_VMEM_BYTES - 8 * 1024 * 1024

ATTN_TQ = 256
ATTN_KEYS_ONLINE = 256
ATTN_KEYS_TWO_PASS = 512

_QA, _KA, _VA = 0, 8, 10
_QB, _KB, _VB = 12, 20, 22
_QR, _KR, _VR, _GR = 24, 32, 40, 48
_N_HEADS_QKV = 56
_N_HEADS_ACT = 48


def _params(*sem):
    return pltpu.CompilerParams(dimension_semantics=sem, vmem_limit_bytes=VMEM_LIMIT)


def _tile(n, pref, unit):
    t = min(pref, n) // unit * unit
    while t > unit and n % t:
        t -= unit
    assert t >= unit and n % t == 0, (n, pref, unit)
    return t


def _sigmoid(x):
    return 1.0 / (1.0 + jnp.exp(-x))


def _silu(x):
    return x * _sigmoid(x)


def _mod_kernel(c_ref, w_ref, b_ref, o_ref):
    a = _silu(c_ref[...]).astype(BF16)
    o_ref[...] = jnp.dot(a, w_ref[...].astype(BF16), preferred_element_type=F32) + b_ref[...]


def _modulation(c_rows, w_mod, b_mod):
    depth, d, n = w_mod.shape
    rows = c_rows.shape[0]
    tn = _tile(n, 1024, 128)
    return pl.pallas_call(
        _mod_kernel,
        grid=(depth, n // tn),
        in_specs=[pl.BlockSpec((rows, d), lambda l, j: (0, 0)),
                  pl.BlockSpec((None, d, tn), lambda l, j: (l, 0, j)),
                  pl.BlockSpec((None, 1, tn), lambda l, j: (l, 0, j))],
        out_specs=pl.BlockSpec((None, rows, tn), lambda l, j: (l, 0, j)),
        out_shape=jax.ShapeDtypeStruct((depth, rows, n), F32),
        compiler_params=_params("parallel", "parallel"),
        name="modulation",
    )(c_rows, w_mod, b_mod.reshape(depth, 1, n))


def _modulate_kernel(x_ref, c_ref, sc_ref, sh_ref, lat_ref, h_ref, *, n_x_tiles):
    def emit(src_ref):
        v = src_ref[...]
        lat_ref[...] = v
        h_ref[...] = (v * (1.0 + sc_ref[...]) + sh_ref[...]).astype(h_ref.dtype)

    @pl.when(pl.program_id(0) < n_x_tiles)
    def _():
        emit(x_ref)

    @pl.when(pl.program_id(0) >= n_x_tiles)
    def _():
        emit(c_ref)


def _modulate(x, ctx, mod, k_scale, k_shift, group_of):
    d = x.shape[1]
    tm = 256
    nx, nc = x.shape[0] // tm, ctx.shape[0] // tm
    assert x.shape[0] % tm == 0 and ctx.shape[0] % tm == 0
    m = x.shape[0] + ctx.shape[0]
    vec = lambda k: pl.BlockSpec((None, None, 1, d), lambda i: (k, group_of(i * tm), 0, 0))
    row = pl.BlockSpec((tm, d), lambda i: (i, 0))
    return pl.pallas_call(
        functools.partial(_modulate_kernel, n_x_tiles=nx),
        grid=(nx + nc,),
        in_specs=[pl.BlockSpec((tm, d), lambda i: (jnp.minimum(i, nx - 1), 0)),
                  pl.BlockSpec((tm, d), lambda i: (jnp.maximum(i - nx, 0), 0)),
                  vec(k_scale), vec(k_shift)],
        out_specs=[row, row],
        out_shape=[jax.ShapeDtypeStruct((m, d), F32), jax.ShapeDtypeStruct((m, d), BF16)],
        compiler_params=_params("arbitrary"),
        name="modulate",
    )(x, ctx, mod, mod)


def _staged_phases(j, stage, compute):
    @pl.when(j == 0)
    def _():
        stage(0)

    for parity in (0, 1):
        @pl.when((j > 0) & (j % 2 == parity))
        def _(parity=parity):
            compute(1 - parity)
            stage(parity)


def _mm_kernel(a_ref, w_ref, *rest, kc, tm, has_bias, resid):
    b_ref = rest[0] if has_bias else None
    o_ref, wbs = rest[-3], rest[-2:]
    j, i = pl.program_id(0), pl.program_id(1)

    def stage(s):
        wbs[s][pl.ds(pl.multiple_of(i * kc, kc), kc), :] = w_ref[...].astype(BF16)

    def compute(s):
        acc = jnp.dot(a_ref[...], wbs[s][...], preferred_element_type=F32)
        if has_bias:
            acc = _sigmoid(acc + b_ref[...])
        if resid is not None:
            alpha, bounds = resid
            lat_ref, gates_ref = rest[-5], rest[-4]
            row = i * tm + lax.broadcasted_iota(jnp.int32, (tm, 1), 0)
            gate = gates_ref[len(bounds):len(bounds) + 1, :]
            for g in reversed(range(len(bounds))):
                gate = jnp.where(row < bounds[g], gates_ref[g:g + 1, :], gate)
            acc = alpha * lat_ref[...] + gate * acc
        o_ref[...] = acc.astype(o_ref.dtype)

    _staged_phases(j, stage, compute)


def _mm_resid_kernel(a_ref, w_ref, lat_ref, gate_ref, o_ref, *, alpha):
    acc = jnp.dot(a_ref[...], w_ref[...], preferred_element_type=F32)
    o_ref[...] = alpha * lat_ref[...] + gate_ref[...] * acc


def _matmul_a_resident(a, w, m, lat, mod, k_gate, group_of, alpha, *, tm=512, tn=512, name="matmul"):
    k = a.shape[1]
    n = w.shape[-1]
    tm = _tile(m, tm, 8)
    tn = _tile(n, tn, 128)
    return pl.pallas_call(
        functools.partial(_mm_resid_kernel, alpha=alpha),
        grid=(m // tm, n // tn),
        in_specs=[pl.BlockSpec((tm, k), lambda i, j: (i, 0)),
                  pl.BlockSpec((k, tn), lambda i, j: (0, j)),
                  pl.BlockSpec((tm, tn), lambda i, j: (i, j)),
                  pl.BlockSpec((None, None, 1, tn), lambda i, j: (k_gate, group_of(i * tm), 0, j))],
        out_specs=pl.BlockSpec((tm, tn), lambda i, j: (i, j)),
        out_shape=jax.ShapeDtypeStruct((m, n), F32),
        compiler_params=_params("parallel", "arbitrary"),
        name=name,
    )(a, w, lat, mod)


_MM_ROW_TILES = 8


def _matmul(a, w, layer, n0, n, m, out_dtype, *, tn=1024, bias=None, resid=None, name="matmul"):
    k = a.shape[1]
    ni = _MM_ROW_TILES
    tm, kc = m // ni, k // ni
    assert m % ni == 0 and tm % 8 == 0 and k % ni == 0 and kc % 16 == 0
    tn = _tile(math.gcd(n, n0) if n0 else n, tn, 128)
    j0, nj = n0 // tn, n // tn
    row = lambda j, i: jnp.where(j == 0, 0, i)
    col = lambda j: jnp.maximum(j - 1, 0)
    in_specs = [pl.BlockSpec((tm, k), lambda j, i: (row(j, i), 0)),
                pl.BlockSpec((None, kc, tn),
                             lambda j, i: (layer, jnp.where(j == nj, ni - 1, i), jnp.minimum(j, nj - 1) + j0))]
    args = [a, w]
    if bias is not None:
        in_specs.append(pl.BlockSpec((None, 1, tn), lambda j, i: (layer, 0, col(j))))
        args.append(bias)
    static_resid = None
    if resid is not None:
        lat, gates, alpha, bounds = resid
        in_specs += [pl.BlockSpec((tm, tn), lambda j, i: (row(j, i), col(j))),
                     pl.BlockSpec((gates.shape[0], tn), lambda j, i: (0, col(j)))]
        args += [lat, gates]
        static_resid = (alpha, tuple(bounds))
    return pl.pallas_call(
        functools.partial(_mm_kernel, kc=kc, tm=tm, has_bias=bias is not None, resid=static_resid),
        grid=(nj + 1, ni),
        in_specs=in_specs,
        out_specs=pl.BlockSpec((tm, tn), lambda j, i: (row(j, i), col(j))),
        out_shape=jax.ShapeDtypeStruct((m, n), out_dtype),
        scratch_shapes=[pltpu.VMEM((k, tn), BF16)] * 2,
        compiler_params=_params("arbitrary", "arbitrary"),
        name=name,
    )(*args)


def _rope(x, c, sa, sb):
    return x * c + pltpu.roll(x, 96, 1) * sa + pltpu.roll(x, 32, 1) * sb


def _rms(x, g):
    return x * lax.rsqrt(jnp.mean(x * x, axis=-1, keepdims=True) + RMS_EPS) * g


def _prep_kernel(p_ref, c_ref, sa_ref, sb_ref, qg_ref, kg_ref, act_ref, krf_ref, grs_ref):
    c, sa, sb = c_ref[...], sa_ref[...], sb_ref[...]
    qg, kg = qg_ref[...], kg_ref[...]
    k_scale = HEAD_DIM ** -0.5
    for h in range(_N_HEADS_QKV):
        cols = slice(h * HEAD_DIM, (h + 1) * HEAD_DIM)
        x = p_ref[:, cols]
        if h < _KA + A_KV_HEADS:
            y = _rope(x, c, sa, sb)
        elif h < _QB:
            y = x
        elif h < _KB:
            y = _rope(_rms(x, qg), c, sa, sb)
        elif h < _VB:
            y = _rope(_rms(x, kg), c, sa, sb)
        elif h < _QR:
            y = x
        elif h < _KR:
            y = _rope(x, c, sa, sb)
        elif h < _VR:
            y = _rope(x, c, sa, sb) * k_scale
            krf_ref[:, (h - _KR) * HEAD_DIM:(h - _KR + 1) * HEAD_DIM] = y
        elif h < _GR:
            y = x
        else:
            grs_ref[:, (h - _GR) * HEAD_DIM:(h - _GR + 1) * HEAD_DIM] = _silu(x)
            continue
        act_ref[:, cols] = y.astype(BF16)


def _prep(p, rope_c, rope_sa, rope_sb, qg, kg):
    m = p.shape[0]
    tm = 256
    row = lambda w: pl.BlockSpec((tm, w), lambda i: (i, 0))
    vec = pl.BlockSpec((1, HEAD_DIM), lambda i: (0, 0))
    return pl.pallas_call(
        _prep_kernel,
        grid=(m // tm,),
        in_specs=[row(_N_HEADS_QKV * HEAD_DIM), row(HEAD_DIM), row(HEAD_DIM), row(HEAD_DIM), vec, vec],
        out_specs=[row(_N_HEADS_ACT * HEAD_DIM), row(C_HEADS * HEAD_DIM), row(C_HEADS * HEAD_DIM)],
        out_shape=[jax.ShapeDtypeStruct((m, _N_HEADS_ACT * HEAD_DIM), BF16),
                   jax.ShapeDtypeStruct((m, C_HEADS * HEAD_DIM), F32),
                   jax.ShapeDtypeStruct((m, C_HEADS * HEAD_DIM), F32)],
        compiler_params=_params("parallel"),
        name="prep",
    )(p, rope_c, rope_sa, rope_sb, qg, kg)


def _attn_head(q_ref, kv, sink_vals, o_ref, o_col0, *, n, bands, tq, seq, group, kchunk, online):
    n_seg = len(bands)
    scale = HEAD_DIM ** -0.5
    log2e = math.log2(math.e)
    q = jnp.concatenate([q_ref[:, g * HEAD_DIM:(g + 1) * HEAD_DIM] for g in range(group)], axis=0)

    def raw_scores(s, c0, c1):
        sc = lax.dot_general(q, kv[2 * s][c0:c1, :], (((1,), (1,)), ((), ())), preferred_element_type=F32)
        if bands[s] is not None:
            qpos = n * tq + lax.broadcasted_iota(jnp.int32, (tq, c1 - c0), 0)
            kpos = n * tq + (bands[s] + c0) + lax.broadcasted_iota(jnp.int32, (tq, c1 - c0), 1)
            valid = (jnp.abs(qpos - kpos) <= WINDOW) & (kpos >= 0) & (kpos < seq)
            sc = jnp.where(jnp.concatenate([valid] * group, axis=0), sc, NEG_INF)
        return sc

    chunks = [(s, c0, min(c0 + kchunk, kv[2 * s].shape[0]))
              for s in range(n_seg) for c0 in range(0, kv[2 * s].shape[0], kchunk)]
    if online:
        c2 = scale * log2e
        sink2 = None
        if sink_vals is not None:
            sink2 = jnp.concatenate([jnp.full((tq, 1), v, F32) for v in sink_vals], axis=0) * log2e
        m2, den, out = sink2, None, None
        for s, c0, c1 in chunks:
            sc = raw_scores(s, c0, c1)
            cm = jnp.max(sc, axis=-1, keepdims=True) * c2
            m_new = cm if m2 is None else jnp.maximum(m2, cm)
            e = jnp.exp2(sc * c2 - m_new)
            r = jnp.sum(e, axis=-1, keepdims=True)
            o = jnp.dot(e.astype(BF16), kv[2 * s + 1][c0:c1, :], preferred_element_type=F32)
            if den is None:
                den, out = r, o
            else:
                alpha = jnp.exp2(m2 - m_new)
                den, out = den * alpha + r, out * alpha + o
            m2 = m_new
        if sink2 is not None:
            den = den + jnp.exp2(sink2 - m2)
        out = out / den
        for g in range(group):
            c = o_col0 + g * HEAD_DIM
            o_ref[:, c:c + HEAD_DIM] = out[g * tq:(g + 1) * tq].astype(o_ref.dtype)
        return

    scores = [raw_scores(*ch) for ch in chunks]
    mx = functools.reduce(jnp.maximum, [jnp.max(sc, axis=-1, keepdims=True) for sc in scores])
    mx = mx * scale
    if sink_vals is not None:
        sink = jnp.concatenate([jnp.full((tq, 1), v, F32) for v in sink_vals], axis=0)
        mx = jnp.maximum(mx, sink)
    mx2 = mx * log2e
    den, out = None, None
    for (s, c0, c1), sc in zip(chunks, scores):
        e = jnp.exp2(sc * (scale * log2e) - mx2)
        r = jnp.sum(e, axis=-1, keepdims=True)
        o = jnp.dot(e.astype(BF16), kv[2 * s + 1][c0:c1, :], preferred_element_type=F32)
        den = r if den is None else den + r
        out = o if out is None else out + o
    if sink_vals is not None:
        den = den + jnp.exp2(sink * log2e - mx2)
    out = out / den
    for g in range(group):
        c = o_col0 + g * HEAD_DIM
        o_ref[:, c:c + HEAD_DIM] = out[g * tq:(g + 1) * tq].astype(o_ref.dtype)


def _attn_kernel(*refs, n_kv, n_seg, bands, has_sink, tq, seq, group, kchunk, online):
    n = pl.program_id(1)
    sink_ref = refs[-2] if has_sink else None
    o_ref = refs[-1]
    for h in range(n_kv):
        kv = refs[n_kv + h * 2 * n_seg:n_kv + (h + 1) * 2 * n_seg]
        sink_vals = [sink_ref[h * group + g] for g in range(group)] if has_sink else None
        _attn_head(refs[h], kv, sink_vals, o_ref, h * group * HEAD_DIM,
                   n=n, bands=bands, tq=tq, seq=seq, group=group, kchunk=kchunk, online=online)


def _attention(act, segs, *, n_batch, q_rows0, q_len, q_head0, n_kv, group, tq, sink, name, online=False):
    gw = group * HEAD_DIM
    nq = q_len // tq
    assert q_rows0 % tq == 0 and q_len % tq == 0 and q_head0 % group == 0
    in_specs, args = [], []
    for h in range(n_kv):
        in_specs.append(pl.BlockSpec(
            (tq, gw), lambda b, n, h=h: (q_rows0 // tq + b * nq + n, q_head0 // group + h)))
        args.append(act)
    for h in range(n_kv):
        for rows, row_fn, k0, v0, _ in segs:
            for c0 in (k0, v0):
                in_specs.append(pl.BlockSpec(
                    (rows, HEAD_DIM), lambda b, n, row_fn=row_fn, c=c0 + h: (row_fn(b, n), c)))
                args.append(act)
    if sink is not None:
        in_specs.append(pl.BlockSpec(memory_space=pltpu.SMEM))
        args.append(sink)
    body = functools.partial(
        _attn_kernel, n_kv=n_kv, n_seg=len(segs), bands=tuple(s[4] for s in segs),
        has_sink=sink is not None, tq=tq, seq=q_len, group=group,
        kchunk=ATTN_KEYS_ONLINE if online else ATTN_KEYS_TWO_PASS, online=online)
    return pl.pallas_call(
        body,
        grid=(n_batch, nq),
        in_specs=in_specs,
        out_specs=pl.BlockSpec((tq, n_kv * gw), lambda b, n: (b * nq + n, 0)),
        out_shape=jax.ShapeDtypeStruct((n_batch * q_len, n_kv * gw), BF16),
        compiler_params=_params("parallel", "arbitrary"),
        name=name,
    )(*args)


def _retention_kernel(lg_ref, *refs, n_batch):
    c = RET_CHUNK
    of_ref, ob_ref, state_ref = refs[-3:]

    @pl.when(pl.program_id(0) == 0)
    def _():
        state_ref[...] = jnp.zeros_like(state_ref)

    row = lax.broadcasted_iota(jnp.int32, (c, c), 0)
    col = lax.broadcasted_iota(jnp.int32, (c, c), 1)
    pos = lax.broadcasted_iota(jnp.int32, (c, 1), 0).astype(F32)
    for d, o_ref in enumerate((of_ref, ob_ref)):
        diff = (row - col) if d == 0 else (col - row)
        keep = (diff >= 0) if d == 0 else (diff > 0)
        dist = jnp.maximum(diff, 0).astype(F32)
        q_pow = (pos + 1.0) if d == 0 else (c - pos)
        k_pow = (c - 1.0 - pos) if d == 0 else pos
        for h in range(C_HEADS):
            cols = slice(h * HEAD_DIM, (h + 1) * HEAD_DIM)
            lg = lg_ref[d, h]
            d_intra = jnp.where(keep, jnp.exp(lg * dist), 0.0)
            d_q = jnp.exp(lg * q_pow)
            d_k = jnp.exp(lg * k_pow)
            d_c = jnp.exp(jnp.full((1, 1), lg * c, F32))
            for b in range(n_batch):
                q_ref, k_ref, kfull_ref, v_ref = refs[(2 * b + d) * 4:(2 * b + d) * 4 + 4]
                q, k, v = q_ref[:, cols], k_ref[:, cols], v_ref[:, cols]
                state = state_ref[b, d, h]
                att = lax.dot_general(q, k, (((1,), (1,)), ((), ())), preferred_element_type=F32) * d_intra
                o = (jnp.dot(att.astype(BF16), v, preferred_element_type=F32)
                     + jnp.dot(q, state.astype(BF16), preferred_element_type=F32) * d_q)
                o_ref[b, :, cols] = o
                kd = (kfull_ref[:, cols] * d_k).T.astype(BF16)
                state_ref[b, d, h] = state * d_c + jnp.dot(kd, v, preferred_element_type=F32)


def _retention(act, krf, log_gamma, *, n_batch, seq, ctx_len):
    c = RET_CHUNK
    w = C_HEADS * HEAD_DIM
    n_ctx, n_lat = ctx_len // c, seq // c
    n_steps = n_ctx + n_lat
    ctx0 = n_batch * seq // c

    def fwd_pos(t):
        return t

    def bwd_pos(t):
        return jnp.where(t < n_ctx, n_ctx - 1 - t, n_steps - 1 - (t - n_ctx))

    def global_chunk(b, p):
        return jnp.where(p < n_ctx, ctx0 + b * n_ctx + p, b * n_lat + p - n_ctx)

    def spec(b, pos_fn, col_block):
        return pl.BlockSpec((c, w), lambda t: (global_chunk(b, pos_fn(t)), col_block))

    qc, kc, vc = _QR * HEAD_DIM // w, _KR * HEAD_DIM // w, _VR * HEAD_DIM // w
    in_specs, args = [pl.BlockSpec(memory_space=pltpu.SMEM)], [log_gamma]
    for b in range(n_batch):
        for pos_fn in (fwd_pos, bwd_pos):
            in_specs += [spec(b, pos_fn, qc), spec(b, pos_fn, kc), spec(b, pos_fn, 0), spec(b, pos_fn, vc)]
            args += [act, act, krf, act]
    out_shape = jax.ShapeDtypeStruct((n_batch, ctx_len + seq, w), F32)
    return pl.pallas_call(
        functools.partial(_retention_kernel, n_batch=n_batch),
        grid=(n_steps,),
        in_specs=in_specs,
        out_specs=[pl.BlockSpec((n_batch, c, w), lambda t: (0, fwd_pos(t), 0)),
                   pl.BlockSpec((n_batch, c, w), lambda t: (0, bwd_pos(t), 0))],
        out_shape=[out_shape, out_shape],
        scratch_shapes=[pltpu.VMEM((n_batch, 2, C_HEADS, HEAD_DIM, HEAD_DIM), F32)],
        compiler_params=_params("arbitrary"),
        name="retention",
    )(*args)


def _gn_gate_kernel(of_ref, ob_ref, grs_ref, g_ref, o_ref):
    for h in range(C_HEADS):
        cols = slice(h * HEAD_DIM, (h + 1) * HEAD_DIM)
        o = of_ref[:, cols] + ob_ref[:, cols]
        mu = jnp.mean(o, axis=-1, keepdims=True)
        var = jnp.mean(jnp.square(o - mu), axis=-1, keepdims=True)
        y = (o - mu) * lax.rsqrt(var + GN_EPS) * g_ref[:, cols]
        o_ref[:, cols] = (grs_ref[:, cols] * y).astype(o_ref.dtype)


def _gn_gate(o_f, o_b, grs, gn_g, layer, m, *, lat_rows, seq, ctx_len):
    w = o_f.shape[-1]
    tm = _tile(ctx_len, 256, 8)

    def scan_block(i):
        r = i * tm
        in_lat = r < lat_rows
        b = jnp.where(in_lat, r // seq, (r - lat_rows) // ctx_len)
        off = jnp.where(in_lat, ctx_len + r % seq, (r - lat_rows) % ctx_len)
        return b, off // tm, 0

    row = pl.BlockSpec((tm, w), lambda i: (i, 0))
    scan = pl.BlockSpec((None, tm, w), scan_block)
    return pl.pallas_call(
        _gn_gate_kernel,
        grid=(m // tm,),
        in_specs=[scan, scan, row, pl.BlockSpec((None, 1, w), lambda i: (layer, 0, 0))],
        out_specs=row,
        out_shape=jax.ShapeDtypeStruct((m, w), BF16),
        compiler_params=_params("parallel"),
        name="gn_gate",
    )(o_f, o_b, grs, gn_g)


def _merge_kernel(ya_ref, yb_ref, yc_ref, w_ref, ga_ref, gb_ref, gc_ref, o_ref, *wbs, kc):
    j, i = pl.program_id(0), pl.program_id(1)

    def stage(s):
        wbs[s][:, pl.ds(pl.multiple_of(i * kc, kc), kc), :] = w_ref[...].astype(BF16)

    def compute(s):
        out = None
        for br, (y_ref, g_ref) in enumerate(((ya_ref, ga_ref), (yb_ref, gb_ref), (yc_ref, gc_ref))):
            t = g_ref[...].astype(F32) * jnp.dot(y_ref[...], wbs[s][br], preferred_element_type=F32)
            out = t if out is None else out + t
        o_ref[...] = out.astype(o_ref.dtype)

    _staged_phases(j, stage, compute)


def _merge(y_a, y_b, y_c, gates, w_branch, layer, m):
    d = w_branch.shape[-1]
    bw = w_branch.shape[-2]
    ni = _MM_ROW_TILES
    tm, kc = m // ni, bw // ni
    assert m % ni == 0 and tm % 8 == 0 and bw % ni == 0 and kc % 16 == 0
    tn = _tile(d, 1024, 128)
    nj = d // tn
    row = lambda j, i: jnp.where(j == 0, 0, i)
    col = lambda j: jnp.maximum(j - 1, 0)
    y_spec = pl.BlockSpec((tm, bw), lambda j, i: (row(j, i), 0))
    g_spec = lambda br: pl.BlockSpec((tm, tn), lambda j, i: (row(j, i), br * nj + col(j)))
    return pl.pallas_call(
        functools.partial(_merge_kernel, kc=kc),
        grid=(nj + 1, ni),
        in_specs=[y_spec, y_spec, y_spec,
                  pl.BlockSpec((None, N_BRANCH, kc, tn),
                               lambda j, i: (layer, 0, jnp.where(j == nj, ni - 1, i), jnp.minimum(j, nj - 1))),
                  g_spec(0), g_spec(1), g_spec(2)],
        out_specs=pl.BlockSpec((tm, tn), lambda j, i: (row(j, i), col(j))),
        out_shape=jax.ShapeDtypeStruct((m, d), BF16),
        scratch_shapes=[pltpu.VMEM((N_BRANCH, bw, tn), BF16)] * 2,
        compiler_params=_params("arbitrary", "arbitrary"),
        name="merge",
    )(y_a, y_b, y_c, w_branch, gates, gates, gates)


def _ln_kernel(z_ref, lng_ref, lnb_ref, *rest, emit_h):
    z = z_ref[...]
    mu = jnp.mean(z, axis=-1, keepdims=True)
    var = jnp.mean(jnp.square(z - mu), axis=-1, keepdims=True)
    out = (z - mu) * lax.rsqrt(var + LN_EPS) * lng_ref[...] + lnb_ref[...]
    if emit_h:
        sc_ref, sh_ref, lat_out, h_out = rest
        h_out[...] = (out * (1.0 + sc_ref[...]) + sh_ref[...]).astype(h_out.dtype)
    else:
        lat_out, = rest
    lat_out[...] = out


def _post_ln(z, mod_next, k_scale, k_shift, ln_g, ln_b, layer, m, group_of, group_rows):
    d = z.shape[1]
    tm = _tile(math.gcd(m, group_rows), 512, 256)
    emit_h = mod_next is not None
    row = pl.BlockSpec((tm, d), lambda i: (i, 0))
    vec = lambda k: pl.BlockSpec((None, None, 1, d), lambda i: (k, group_of(i * tm), 0, 0))
    par = pl.BlockSpec((None, 1, d), lambda i: (layer, 0, 0))
    in_specs = [row, par, par]
    args = [z, ln_g, ln_b]
    out_specs = [row]
    out_shape = [jax.ShapeDtypeStruct((m, d), F32)]
    if emit_h:
        in_specs += [vec(k_scale), vec(k_shift)]
        args += [mod_next, mod_next]
        out_specs.append(row)
        out_shape.append(jax.ShapeDtypeStruct((m, d), BF16))
    res = pl.pallas_call(
        functools.partial(_ln_kernel, emit_h=emit_h),
        grid=(m // tm,),
        in_specs=in_specs,
        out_specs=out_specs,
        out_shape=out_shape,
        compiler_params=_params("parallel"),
        name="post_ln",
    )(*args)
    return (res[0], res[1]) if emit_h else (res[0], None)


_HALO = 8
_UP_ROW_TILES = 4


def _up_conv_kernel(a_ref, wg_ref, wu_ref, cw_ref, cb_ref, wd_ref, o_ref, wd_out_ref, *wbs,
                    kc, tm, n_sub, seq_first_rows, seq_last_rows):
    j, i = pl.program_id(0), pl.program_id(1)

    def stage(s):
        wd_out_ref[...] = wd_ref[...].astype(BF16)
        rows = pl.ds(pl.multiple_of(i * kc, kc), kc)
        wbs[s][0, rows, :] = wg_ref[...].astype(BF16)
        wbs[s][1, rows, :] = wu_ref[...].astype(BF16)

    def compute(s):
        ts = tm // n_sub
        n_rows = ts + 2 * _HALO
        for k in range(n_sub):
            r0 = k * ts
            g = jnp.dot(a_ref[r0:r0 + n_rows, :], wbs[s][0], preferred_element_type=F32)
            u = jnp.dot(a_ref[r0 + _HALO:r0 + _HALO + ts, :], wbs[s][1], preferred_element_type=F32)
            g_prev = pltpu.roll(g, 1, 0)[_HALO:_HALO + ts]
            g_next = pltpu.roll(g, n_rows - 1, 0)[_HALO:_HALO + ts]
            row = i * tm + r0 + lax.broadcasted_iota(jnp.int32, (ts, 1), 0)
            has_prev = functools.reduce(lambda x, y: x & y, [row != r for r in seq_first_rows])
            has_next = functools.reduce(lambda x, y: x & y, [row != r for r in seq_last_rows])
            conv = (cb_ref[...] + jnp.where(has_prev, g_prev, 0.0) * cw_ref[0:1, :]
                    + g[_HALO:_HALO + ts] * cw_ref[1:2, :] + jnp.where(has_next, g_next, 0.0) * cw_ref[2:3, :])
            o_ref[r0:r0 + ts, :] = (_silu(conv) * u).astype(o_ref.dtype)

    _staged_phases(j, stage, compute)


def _up_conv_gate(h, w_up, conv_w, conv_b, w_down, layer, m, seq_lens):
    k = h.shape[1]
    f = w_up.shape[-1] // 2
    d_out = w_down.shape[-1]
    ni = _UP_ROW_TILES
    tm, kc = m // ni, k // ni
    assert m % ni == 0 and tm % 8 == 0 and k % ni == 0 and kc % 16 == 0 and sum(seq_lens) == m
    tf = _tile(f, 256, 128)
    nj = f // tf
    wd_rows = f // (nj * ni)
    assert f % (nj * ni) == 0 and wd_rows % 16 == 0
    wd_blk = lambda j, i: jnp.maximum(j - 1, 0) * ni + jnp.where(j == 0, 0, i)
    starts = [sum(seq_lens[:s]) for s in range(len(seq_lens))]
    first_rows = tuple(starts)
    last_rows = tuple(st + n - 1 for st, n in zip(starts, seq_lens))
    edge = jnp.zeros((_HALO, k), h.dtype)
    pieces = []
    for t in range(ni):
        lo, hi = t * tm, (t + 1) * tm
        pieces += [h[lo - _HALO:lo] if t else edge, h[lo:hi], h[hi:hi + _HALO] if hi < m else edge]
    tiles = jnp.concatenate(pieces, axis=0).reshape(ni, tm + 2 * _HALO, k)
    row = lambda j, i: jnp.where(j == 0, 0, i)
    col = lambda j: jnp.maximum(j - 1, 0)
    chunk = lambda j, i: jnp.where(j == nj, ni - 1, i)
    stage = lambda j: jnp.minimum(j, nj - 1)
    n_sub = max(1, tm // 512)
    assert tm % (16 * n_sub) == 0
    body = functools.partial(_up_conv_kernel, kc=kc, tm=tm, n_sub=n_sub,
                             seq_first_rows=first_rows, seq_last_rows=last_rows)
    return pl.pallas_call(
        body,
        grid=(nj + 1, ni),
        in_specs=[pl.BlockSpec((None, tm + 2 * _HALO, k), lambda j, i: (row(j, i), 0, 0)),
                  pl.BlockSpec((None, kc, tf), lambda j, i: (layer, chunk(j, i), stage(j))),
                  pl.BlockSpec((None, kc, tf), lambda j, i: (layer, chunk(j, i), nj + stage(j))),
                  pl.BlockSpec((None, CONV_WIDTH, tf), lambda j, i: (layer, 0, col(j))),
                  pl.BlockSpec((None, 1, tf), lambda j, i: (layer, 0, col(j))),
                  pl.BlockSpec((None, wd_rows, d_out), lambda j, i: (layer, wd_blk(j, i), 0))],
        out_specs=[pl.BlockSpec((tm, tf), lambda j, i: (row(j, i), col(j))),
                   pl.BlockSpec((wd_rows, d_out), lambda j, i: (wd_blk(j, i), 0))],
        out_shape=[jax.ShapeDtypeStruct((m, f), BF16), jax.ShapeDtypeStruct((f, d_out), BF16)],
        scratch_shapes=[pltpu.VMEM((2, k, tf), BF16)] * 2,
        compiler_params=_params("arbitrary", "arbitrary"),
        name="up_conv_gate",
    )(tiles, w_up, w_up, conv_w, conv_b, w_down)


def _rope_tables(seq, n_batch, n_ctx_rows):
    rows = seq // GRID_W
    r_idx, c_idx = jnp.meshgrid(jnp.arange(rows, dtype=F32), jnp.arange(GRID_W, dtype=F32), indexing='ij')
    axis_dim = HEAD_DIM // 2
    inv_freq = ROPE_BASE ** (-jnp.arange(0, axis_dim, 2, dtype=F32) / axis_dim)
    ang_r = r_idx.reshape(-1)[:, None] * inv_freq
    ang_c = c_idx.reshape(-1)[:, None] * inv_freq
    cr, sr, cc, sn = jnp.cos(ang_r), jnp.sin(ang_r), jnp.cos(ang_c), jnp.sin(ang_c)
    zero = jnp.zeros_like(sr)
    c = jnp.concatenate([cr, cr, cc, cc], axis=-1)
    sa = jnp.concatenate([-sr, zero, -sn, zero], axis=-1)
    sb = jnp.concatenate([zero, sr, zero, sn], axis=-1)

    def full(t, fill):
        return jnp.concatenate([jnp.tile(t, (n_batch, 1)), jnp.full((n_ctx_rows, HEAD_DIM), fill, F32)], axis=0)

    return full(c, 1.0), full(sa, 0.0), full(sb, 0.0)


def kernel(x, c, ctx, c_ctx, w_mod, b_mod, w_in, b_gate, w_branch, w_o, attn_sink, q_norm_g, k_norm_g,
           ret_decay, ret_gn_g, ln1_g, ln1_b, w_up, ffn_conv_w, ffn_conv_b, w_down, ln2_g, ln2_b):
    n_batch, seq, d = x.shape
    ctx_len = ctx.shape[1]
    depth = w_mod.shape[0]
    d_ff = w_down.shape[1]
    lat_rows = n_batch * seq
    m_all = lat_rows + n_batch * ctx_len
    alpha = (2 * depth) ** 0.25
    qkv_w = _N_HEADS_QKV * HEAD_DIM
    assert seq % 256 == 0 and ctx_len % 256 == 0 and lat_rows % ctx_len == 0

    def group_of(row_start):
        return jnp.minimum(row_start // seq, n_batch)

    group_bounds = [seq * (b + 1) for b in range(n_batch)]

    c_rows = jnp.concatenate([c, c_ctx[None], jnp.zeros((8 - n_batch - 1, d), F32)], axis=0)
    mod = _modulation(c_rows, w_mod, b_mod)
    mod = mod.reshape(depth, 8, 6, d).transpose(0, 2, 1, 3)[:, :, :, None, :]
    SH1, SC1, G1, SH2, SC2, G2 = range(6)

    rope_c, rope_sa, rope_sb = _rope_tables(seq, n_batch, n_batch * ctx_len)
    log_gamma = -jnp.exp(ret_decay.astype(F32))
    b_gate3 = b_gate.reshape(depth, 1, -1)
    gn_g3 = ret_gn_g.reshape(depth, 1, -1)
    ln1_g3, ln1_b3 = ln1_g.reshape(depth, 1, d), ln1_b.reshape(depth, 1, d)
    ln2_g3, ln2_b3 = ln2_g.reshape(depth, 1, d), ln2_b.reshape(depth, 1, d)
    conv_b3 = ffn_conv_b.reshape(depth, 1, d_ff)

    lat, h = _modulate(x.reshape(lat_rows, d), ctx.reshape(n_batch * ctx_len, d), mod[0], SC1, SH1, group_of)

    for l in range(depth):
        last = l == depth - 1
        m = lat_rows if last else m_all

        p = _matmul(h, w_in, l, 0, qkv_w, m_all, F32, name="in_proj_qkv")
        gates = _matmul(h, w_in, l, qkv_w, N_BRANCH * d, m, BF16, bias=b_gate3, name="in_proj_gate")
        act, krf, grs = _prep(p, rope_c, rope_sa, rope_sb, q_norm_g[l][None], k_norm_g[l][None])

        tq = ATTN_TQ
        r128 = tq // 128
        lat128 = seq // 128
        ctx_seg = lambda k0, v0: (ctx_len, lambda b, n: lat_rows // ctx_len + b, k0, v0, None)
        segs_a = [
            (128, lambda b, n: b * lat128 + jnp.maximum(n * r128 - 1, 0), _KA, _VA, -128),
            (tq, lambda b, n: b * (seq // tq) + n, _KA, _VA, 0),
            (128, lambda b, n: b * lat128 + jnp.minimum((n + 1) * r128, lat128 - 1), _KA, _VA, tq),
            ctx_seg(_KA, _VA),
        ]
        attn = functools.partial(_attention, act, n_batch=n_batch)
        y_a = attn(segs_a, q_rows0=0, q_len=seq, q_head0=_QA, n_kv=A_KV_HEADS,
                   group=A_HEADS // A_KV_HEADS, tq=tq, sink=attn_sink[l], name="attn_window")
        segs_b = [ctx_seg(_KB, _VB), (seq, lambda b, n: b, _KB, _VB, None)]
        y_b = attn(segs_b, q_rows0=0, q_len=seq, q_head0=_QB, n_kv=B_KV_HEADS,
                   group=B_HEADS // B_KV_HEADS, tq=ATTN_TQ, sink=None, name="attn_dense", online=True)
        if not last:
            y_a_c = attn([ctx_seg(_KA, _VA)], q_rows0=lat_rows, q_len=ctx_len, q_head0=_QA,
                         n_kv=A_KV_HEADS, group=A_HEADS // A_KV_HEADS, tq=ctx_len, sink=attn_sink[l],
                         name="attn_ctx_a")
            y_b_c = attn([ctx_seg(_KB, _VB)], q_rows0=lat_rows, q_len=ctx_len, q_head0=_QB,
                         n_kv=B_KV_HEADS, group=B_HEADS // B_KV_HEADS, tq=ctx_len, sink=None,
                         name="attn_ctx_b")
            y_a = jnp.concatenate([y_a, y_a_c], axis=0)
            y_b = jnp.concatenate([y_b, y_b_c], axis=0)

        o_f, o_b = _retention(act, krf, log_gamma[l], n_batch=n_batch, seq=seq, ctx_len=ctx_len)
        y_c = _gn_gate(o_f, o_b, grs, gn_g3, l, m, lat_rows=lat_rows, seq=seq, ctx_len=ctx_len)

        merged = _merge(y_a, y_b, y_c, gates, w_branch, l, m)
        g1_rows = mod[l, G1].reshape(8, d)
        z = _matmul(merged, w_o, l, 0, d, m, F32, tn=512, name="out_proj",
                    resid=(lat, g1_rows, alpha, group_bounds))
        lat, h2 = _post_ln(z, mod[l], SC2, SH2, ln1_g3, ln1_b3, l, m, group_of, seq)

        seq_lens = [seq] * n_batch + ([] if last else [ctx_len] * n_batch)
        a, w_down_bf = _up_conv_gate(h2, w_up, ffn_conv_w, conv_b3, w_down, l, m, seq_lens)
        z2 = _matmul_a_resident(a, w_down_bf, m, lat, mod[l], G2, group_of, alpha, name="down_proj")
        nxt = None if last else mod[l + 1]
        lat, h = _post_ln(z2, nxt, SC1, SH1, ln2_g3, ln2_b3, l, m, group_of, seq)

    return lat.reshape(n_batch, seq, d)
```

```python
import functools
import math

import jax
import jax.numpy as jnp
from jax import lax
from jax.experimental import pallas as pl
from jax.experimental.pallas import tpu as pltpu

F32 = jnp.float32
BF16 = jnp.bfloat16

GRID_W = 64
HEAD_DIM = 128
A_HEADS = 8
A_KV_HEADS = 2
B_HEADS = 8
B_KV_HEADS = 2
C_HEADS = 8
WINDOW = 128
RET_CHUNK = 256
N_BRANCH = 3
CONV_WIDTH = 3
ROPE_BASE = 10000.0
LN_EPS = 1e-5
RMS_EPS = 1e-6
GN_EPS = 1e-5
NEG_INF = -1e30

V7X_VMEM_BYTES = 64 * 1024 * 1024
VMEM_LIMIT = V7X_VMEM_BYTES - 8 * 1024 * 1024

ATTN_TQ = 256
ATTN_KEYS_ONLINE = 256
ATTN_KEYS_TWO_PASS = 512

_QA, _KA, _VA = 0, 8, 10
_QB, _KB, _VB = 12, 20, 22
_QR, _KR, _VR, _GR = 24, 32, 40, 48
_N_HEADS_QKV = 56
_N_HEADS_ACT = 48


def _params(*sem):
    return pltpu.CompilerParams(dimension_semantics=sem, vmem_limit_bytes=VMEM_LIMIT)


def _tile(n, pref, unit):
    t = min(pref, n) // unit * unit
    while t > unit and n % t:
        t -= unit
    assert t >= unit and n % t == 0, (n, pref, unit)
    return t


def _sigmoid(x):
    return 1.0 / (1.0 + jnp.exp(-x))


def _silu(x):
    return x * _sigmoid(x)


def _mod_kernel(c_ref, w_ref, b_ref, o_ref):
    a = _silu(c_ref[...]).astype(BF16)
    o_ref[...] = jnp.dot(a, w_ref[...].astype(BF16), preferred_element_type=F32) + b_ref[...]


def _modulation(c_rows, w_mod, b_mod):
    depth, d, n = w_mod.shape
    rows = c_rows.shape[0]
    tn = _tile(n, 1024, 128)
    return pl.pallas_call(
        _mod_kernel,
        grid=(depth, n // tn),
        in_specs=[pl.BlockSpec((rows, d), lambda l, j: (0, 0)),
                  pl.BlockSpec((None, d, tn), lambda l, j: (l, 0, j)),
                  pl.BlockSpec((None, 1, tn), lambda l, j: (l, 0, j))],
        out_specs=pl.BlockSpec((None, rows, tn), lambda l, j: (l, 0, j)),
        out_shape=jax.ShapeDtypeStruct((depth, rows, n), F32),
        compiler_params=_params("parallel", "parallel"),
        name="modulation",
    )(c_rows, w_mod, b_mod.reshape(depth, 1, n))


def _modulate_kernel(x_ref, c_ref, sc_ref, sh_ref, lat_ref, h_ref, *, n_x_tiles):
    def emit(src_ref):
        v = src_ref[...]
        lat_ref[...] = v
        h_ref[...] = (v * (1.0 + sc_ref[...]) + sh_ref[...]).astype(h_ref.dtype)

    @pl.when(pl.program_id(0) < n_x_tiles)
    def _():
        emit(x_ref)

    @pl.when(pl.program_id(0) >= n_x_tiles)
    def _():
        emit(c_ref)


def _modulate(x, ctx, mod, k_scale, k_shift, group_of):
    d = x.shape[1]
    tm = 256
    nx, nc = x.shape[0] // tm, ctx.shape[0] // tm
    assert x.shape[0] % tm == 0 and ctx.shape[0] % tm == 0
    m = x.shape[0] + ctx.shape[0]
    vec = lambda k: pl.BlockSpec((None, None, 1, d), lambda i: (k, group_of(i * tm), 0, 0))
    row = pl.BlockSpec((tm, d), lambda i: (i, 0))
    return pl.pallas_call(
        functools.partial(_modulate_kernel, n_x_tiles=nx),
        grid=(nx + nc,),
        in_specs=[pl.BlockSpec((tm, d), lambda i: (jnp.minimum(i, nx - 1), 0)),
                  pl.BlockSpec((tm, d), lambda i: (jnp.maximum(i - nx, 0), 0)),
                  vec(k_scale), vec(k_shift)],
        out_specs=[row, row],
        out_shape=[jax.ShapeDtypeStruct((m, d), F32), jax.ShapeDtypeStruct((m, d), BF16)],
        compiler_params=_params("arbitrary"),
        name="modulate",
    )(x, ctx, mod, mod)


def _staged_phases(j, stage, compute):
    @pl.when(j == 0)
    def _():
        stage(0)

    for parity in (0, 1):
        @pl.when((j > 0) & (j % 2 == parity))
        def _(parity=parity):
            compute(1 - parity)
            stage(parity)


def _mm_kernel(a_ref, w_ref, *rest, kc, tm, has_bias, resid):
    b_ref = rest[0] if has_bias else None
    o_ref, wbs = rest[-3], rest[-2:]
    j, i = pl.program_id(0), pl.program_id(1)

    def stage(s):
        wbs[s][pl.ds(pl.multiple_of(i * kc, kc), kc), :] = w_ref[...].astype(BF16)

    def compute(s):
        acc = jnp.dot(a_ref[...], wbs[s][...], preferred_element_type=F32)
        if has_bias:
            acc = _sigmoid(acc + b_ref[...])
        if resid is not None:
            alpha, bounds = resid
            lat_ref, gates_ref = rest[-5], rest[-4]
            row = i * tm + lax.broadcasted_iota(jnp.int32, (tm, 1), 0)
            gate = gates_ref[len(bounds):len(bounds) + 1, :]
            for g in reversed(range(len(bounds))):
                gate = jnp.where(row < bounds[g], gates_ref[g:g + 1, :], gate)
            acc = alpha * lat_ref[...] + gate * acc
        o_ref[...] = acc.astype(o_ref.dtype)

    _staged_phases(j, stage, compute)


def _mm_resid_kernel(a_ref, w_ref, lat_ref, gate_ref, o_ref, *, alpha):
    acc = jnp.dot(a_ref[...], w_ref[...], preferred_element_type=F32)
    o_ref[...] = alpha * lat_ref[...] + gate_ref[...] * acc


def _matmul_a_resident(a, w, m, lat, mod, k_gate, group_of, alpha, *, tm=512, tn=512, name="matmul"):
    k = a.shape[1]
    n = w.shape[-1]
    tm = _tile(m, tm, 8)
    tn = _tile(n, tn, 128)
    return pl.pallas_call(
        functools.partial(_mm_resid_kernel, alpha=alpha),
        grid=(m // tm, n // tn),
        in_specs=[pl.BlockSpec((tm, k), lambda i, j: (i, 0)),
                  pl.BlockSpec((k, tn), lambda i, j: (0, j)),
                  pl.BlockSpec((tm, tn), lambda i, j: (i, j)),
                  pl.BlockSpec((None, None, 1, tn), lambda i, j: (k_gate, group_of(i * tm), 0, j))],
        out_specs=pl.BlockSpec((tm, tn), lambda i, j: (i, j)),
        out_shape=jax.ShapeDtypeStruct((m, n), F32),
        compiler_params=_params("parallel", "arbitrary"),
        name=name,
    )(a, w, lat, mod)


_MM_ROW_TILES = 8


def _matmul(a, w, layer, n0, n, m, out_dtype, *, tn=1024, bias=None, resid=None, name="matmul"):
    k = a.shape[1]
    ni = _MM_ROW_TILES
    tm, kc = m // ni, k // ni
    assert m % ni == 0 and tm % 8 == 0 and k % ni == 0 and kc % 16 == 0
    tn = _tile(math.gcd(n, n0) if n0 else n, tn, 128)
    j0, nj = n0 // tn, n // tn
    row = lambda j, i: jnp.where(j == 0, 0, i)
    col = lambda j: jnp.maximum(j - 1, 0)
    in_specs = [pl.BlockSpec((tm, k), lambda j, i: (row(j, i), 0)),
                pl.BlockSpec((None, kc, tn),
                             lambda j, i: (layer, jnp.where(j == nj, ni - 1, i), jnp.minimum(j, nj - 1) + j0))]
    args = [a, w]
    if bias is not None:
        in_specs.append(pl.BlockSpec((None, 1, tn), lambda j, i: (layer, 0, col(j))))
        args.append(bias)
    static_resid = None
    if resid is not None:
        lat, gates, alpha, bounds = resid
        in_specs += [pl.BlockSpec((tm, tn), lambda j, i: (row(j, i), col(j))),
                     pl.BlockSpec((gates.shape[0], tn), lambda j, i: (0, col(j)))]
        args += [lat, gates]
        static_resid = (alpha, tuple(bounds))
    return pl.pallas_call(
        functools.partial(_mm_kernel, kc=kc, tm=tm, has_bias=bias is not None, resid=static_resid),
        grid=(nj + 1, ni),
        in_specs=in_specs,
        out_specs=pl.BlockSpec((tm, tn), lambda j, i: (row(j, i), col(j))),
        out_shape=jax.ShapeDtypeStruct((m, n), out_dtype),
        scratch_shapes=[pltpu.VMEM((k, tn), BF16)] * 2,
        compiler_params=_params("arbitrary", "arbitrary"),
        name=name,
    )(*args)


def _rope(x, c, sa, sb):
    return x * c + pltpu.roll(x, 96, 1) * sa + pltpu.roll(x, 32, 1) * sb


def _rms(x, g):
    return x * lax.rsqrt(jnp.mean(x * x, axis=-1, keepdims=True) + RMS_EPS) * g


def _prep_kernel(p_ref, c_ref, sa_ref, sb_ref, qg_ref, kg_ref, act_ref, krf_ref, grs_ref):
    c, sa, sb = c_ref[...], sa_ref[...], sb_ref[...]
    qg, kg = qg_ref[...], kg_ref[...]
    k_scale = HEAD_DIM ** -0.5
    for h in range(_N_HEADS_QKV):
        cols = slice(h * HEAD_DIM, (h + 1) * HEAD_DIM)
        x = p_ref[:, cols]
        if h < _KA + A_KV_HEADS:
            y = _rope(x, c, sa, sb)
        elif h < _QB:
            y = x
        elif h < _KB:
            y = _rope(_rms(x, qg), c, sa, sb)
        elif h < _VB:
            y = _rope(_rms(x, kg), c, sa, sb)
        elif h < _QR:
            y = x
        elif h < _KR:
            y = _rope(x, c, sa, sb)
        elif h < _VR:
            y = _rope(x, c, sa, sb) * k_scale
            krf_ref[:, (h - _KR) * HEAD_DIM:(h - _KR + 1) * HEAD_DIM] = y
        elif h < _GR:
            y = x
        else:
            grs_ref[:, (h - _GR) * HEAD_DIM:(h - _GR + 1) * HEAD_DIM] = _silu(x)
            continue
        act_ref[:, cols] = y.astype(BF16)


def _prep(p, rope_c, rope_sa, rope_sb, qg, kg):
    m = p.shape[0]
    tm = 256
    row = lambda w: pl.BlockSpec((tm, w), lambda i: (i, 0))
    vec = pl.BlockSpec((1, HEAD_DIM), lambda i: (0, 0))
    return pl.pallas_call(
        _prep_kernel,
        grid=(m // tm,),
        in_specs=[row(_N_HEADS_QKV * HEAD_DIM), row(HEAD_DIM), row(HEAD_DIM), row(HEAD_DIM), vec, vec],
        out_specs=[row(_N_HEADS_ACT * HEAD_DIM), row(C_HEADS * HEAD_DIM), row(C_HEADS * HEAD_DIM)],
        out_shape=[jax.ShapeDtypeStruct((m, _N_HEADS_ACT * HEAD_DIM), BF16),
                   jax.ShapeDtypeStruct((m, C_HEADS * HEAD_DIM), F32),
                   jax.ShapeDtypeStruct((m, C_HEADS * HEAD_DIM), F32)],
        compiler_params=_params("parallel"),
        name="prep",
    )(p, rope_c, rope_sa, rope_sb, qg, kg)


def _attn_head(q_ref, kv, sink_vals, o_ref, o_col0, *, n, bands, tq, seq, group, kchunk, online):
    n_seg = len(bands)
    scale = HEAD_DIM ** -0.5
    log2e = math.log2(math.e)
    q = jnp.concatenate([q_ref[:, g * HEAD_DIM:(g + 1) * HEAD_DIM] for g in range(group)], axis=0)

    def raw_scores(s, c0, c1):
        sc = lax.dot_general(q, kv[2 * s][c0:c1, :], (((1,), (1,)), ((), ())), preferred_element_type=F32)
        if bands[s] is not None:
            qpos = n * tq + lax.broadcasted_iota(jnp.int32, (tq, c1 - c0), 0)
            kpos = n * tq + (bands[s] + c0) + lax.broadcasted_iota(jnp.int32, (tq, c1 - c0), 1)
            valid = (jnp.abs(qpos - kpos) <= WINDOW) & (kpos >= 0) & (kpos < seq)
            sc = jnp.where(jnp.concatenate([valid] * group, axis=0), sc, NEG_INF)
        return sc

    chunks = [(s, c0, min(c0 + kchunk, kv[2 * s].shape[0]))
              for s in range(n_seg) for c0 in range(0, kv[2 * s].shape[0], kchunk)]
    if online:
        c2 = scale * log2e
        sink2 = None
        if sink_vals is not None:
            sink2 = jnp.concatenate([jnp.full((tq, 1), v, F32) for v in sink_vals], axis=0) * log2e
        m2, den, out = sink2, None, None
        for s, c0, c1 in chunks:
            sc = raw_scores(s, c0, c1)
            cm = jnp.max(sc, axis=-1, keepdims=True) * c2
            m_new = cm if m2 is None else jnp.maximum(m2, cm)
            e = jnp.exp2(sc * c2 - m_new)
            r = jnp.sum(e, axis=-1, keepdims=True)
            o = jnp.dot(e.astype(BF16), kv[2 * s + 1][c0:c1, :], preferred_element_type=F32)
            if den is None:
                den, out = r, o
            else:
                alpha = jnp.exp2(m2 - m_new)
                den, out = den * alpha + r, out * alpha + o
            m2 = m_new
        if sink2 is not None:
            den = den + jnp.exp2(sink2 - m2)
        out = out / den
        for g in range(group):
            c = o_col0 + g * HEAD_DIM
            o_ref[:, c:c + HEAD_DIM] = out[g * tq:(g + 1) * tq].astype(o_ref.dtype)
        return

    scores = [raw_scores(*ch) for ch in chunks]
    mx = functools.reduce(jnp.maximum, [jnp.max(sc, axis=-1, keepdims=True) for sc in scores])
    mx = mx * scale
    if sink_vals is not None:
        sink = jnp.concatenate([jnp.full((tq, 1), v, F32) for v in sink_vals], axis=0)
        mx = jnp.maximum(mx, sink)
    mx2 = mx * log2e
    den, out = None, None
    for (s, c0, c1), sc in zip(chunks, scores):
        e = jnp.exp2(sc * (scale * log2e) - mx2)
        r = jnp.sum(e, axis=-1, keepdims=True)
        o = jnp.dot(e.astype(BF16), kv[2 * s + 1][c0:c1, :], preferred_element_type=F32)
        den = r if den is None else den + r
        out = o if out is None else out + o
    if sink_vals is not None:
        den = den + jnp.exp2(sink * log2e - mx2)
    out = out / den
    for g in range(group):
        c = o_col0 + g * HEAD_DIM
        o_ref[:, c:c + HEAD_DIM] = out[g * tq:(g + 1) * tq].astype(o_ref.dtype)


def _attn_kernel(*refs, n_kv, n_seg, bands, has_sink, has_into, tq, seq, group, kchunk, online):
    n = pl.program_id(1)
    sink_ref = refs[-2 - int(has_into)] if has_sink else None
    o_ref = refs[-1]
    for h in range(n_kv):
        kv = refs[n_kv + h * 2 * n_seg:n_kv + (h + 1) * 2 * n_seg]
        sink_vals = [sink_ref[h * group + g] for g in range(group)] if has_sink else None
        _attn_head(refs[h], kv, sink_vals, o_ref, h * group * HEAD_DIM,
                   n=n, bands=bands, tq=tq, seq=seq, group=group, kchunk=kchunk, online=online)


def _attention(act, segs, *, n_batch, q_rows0, q_len, q_head0, n_kv, group, tq, sink, name, online=False,
               out_rows, into=None):
    gw = group * HEAD_DIM
    nq = q_len // tq
    assert q_rows0 % tq == 0 and q_len % tq == 0 and q_head0 % group == 0
    in_specs, args = [], []
    for h in range(n_kv):
        in_specs.append(pl.BlockSpec(
            (tq, gw), lambda b, n, h=h: (q_rows0 // tq + b * nq + n, q_head0 // group + h)))
        args.append(act)
    for h in range(n_kv):
        for rows, row_fn, k0, v0, _ in segs:
            for c0 in (k0, v0):
                in_specs.append(pl.BlockSpec(
                    (rows, HEAD_DIM), lambda b, n, row_fn=row_fn, c=c0 + h: (row_fn(b, n), c)))
                args.append(act)
    if sink is not None:
        in_specs.append(pl.BlockSpec(memory_space=pltpu.SMEM))
        args.append(sink)
    aliases = {}
    if into is not None:
        assert into.shape == (out_rows, n_kv * gw) and into.dtype == BF16
        in_specs.append(pl.BlockSpec(memory_space=pl.ANY))
        args.append(into)
        aliases = {len(args) - 1: 0}
    body = functools.partial(
        _attn_kernel, n_kv=n_kv, n_seg=len(segs), bands=tuple(s[4] for s in segs),
        has_sink=sink is not None, has_into=into is not None, tq=tq, seq=q_len, group=group,
        kchunk=ATTN_KEYS_ONLINE if online else ATTN_KEYS_TWO_PASS, online=online)
    return pl.pallas_call(
        body,
        grid=(n_batch, nq),
        in_specs=in_specs,
        out_specs=pl.BlockSpec((tq, n_kv * gw), lambda b, n: (q_rows0 // tq + b * nq + n, 0)),
        out_shape=jax.ShapeDtypeStruct((out_rows, n_kv * gw), BF16),
        input_output_aliases=aliases,
        compiler_params=_params("parallel", "arbitrary"),
        name=name,
    )(*args)


def _retention_kernel(lg_ref, *refs, n_batch):
    c = RET_CHUNK
    of_ref, ob_ref, state_ref = refs[-3:]

    @pl.when(pl.program_id(0) == 0)
    def _():
        state_ref[...] = jnp.zeros_like(state_ref)

    row = lax.broadcasted_iota(jnp.int32, (c, c), 0)
    col = lax.broadcasted_iota(jnp.int32, (c, c), 1)
    pos = lax.broadcasted_iota(jnp.int32, (c, 1), 0).astype(F32)
    for d, o_ref in enumerate((of_ref, ob_ref)):
        diff = (row - col) if d == 0 else (col - row)
        keep = (diff >= 0) if d == 0 else (diff > 0)
        dist = jnp.maximum(diff, 0).astype(F32)
        q_pow = (pos + 1.0) if d == 0 else (c - pos)
        k_pow = (c - 1.0 - pos) if d == 0 else pos
        for h in range(C_HEADS):
            cols = slice(h * HEAD_DIM, (h + 1) * HEAD_DIM)
            lg = lg_ref[d, h]
            d_intra = jnp.where(keep, jnp.exp(lg * dist), 0.0)
            d_q = jnp.exp(lg * q_pow)
            d_k = jnp.exp(lg * k_pow)
            d_c = jnp.exp(jnp.full((1, 1), lg * c, F32))
            for b in range(n_batch):
                q_ref, k_ref, kfull_ref, v_ref = refs[(2 * b + d) * 4:(2 * b + d) * 4 + 4]
                q, k, v = q_ref[:, cols], k_ref[:, cols], v_ref[:, cols]
                state = state_ref[b, d, h]
                att = lax.dot_general(q, k, (((1,), (1,)), ((), ())), preferred_element_type=F32) * d_intra
                o = (jnp.dot(att.astype(BF16), v, preferred_element_type=F32)
                     + jnp.dot(q, state.astype(BF16), preferred_element_type=F32) * d_q)
                o_ref[b, :, cols] = o
                kd = (kfull_ref[:, cols] * d_k).T.astype(BF16)
                state_ref[b, d, h] = state * d_c + jnp.dot(kd, v, preferred_element_type=F32)


def _retention(act, krf, log_gamma, *, n_batch, seq, ctx_len):
    c = RET_CHUNK
    w = C_HEADS * HEAD_DIM
    n_ctx, n_lat = ctx_len // c, seq // c
    n_steps = n_ctx + n_lat
    ctx0 = n_batch * seq // c

    def fwd_pos(t):
        return t

    def bwd_pos(t):
        return jnp.where(t < n_ctx, n_ctx - 1 - t, n_steps - 1 - (t - n_ctx))

    def global_chunk(b, p):
        return jnp.where(p < n_ctx, ctx0 + b * n_ctx + p, b * n_lat + p - n_ctx)

    def spec(b, pos_fn, col_block):
        return pl.BlockSpec((c, w), lambda t: (global_chunk(b, pos_fn(t)), col_block))

    qc, kc, vc = _QR * HEAD_DIM // w, _KR * HEAD_DIM // w, _VR * HEAD_DIM // w
    in_specs, args = [pl.BlockSpec(memory_space=pltpu.SMEM)], [log_gamma]
    for b in range(n_batch):
        for pos_fn in (fwd_pos, bwd_pos):
            in_specs += [spec(b, pos_fn, qc), spec(b, pos_fn, kc), spec(b, pos_fn, 0), spec(b, pos_fn, vc)]
            args += [act, act, krf, act]
    out_shape = jax.ShapeDtypeStruct((n_batch, ctx_len + seq, w), F32)
    return pl.pallas_call(
        functools.partial(_retention_kernel, n_batch=n_batch),
        grid=(n_steps,),
        in_specs=in_specs,
        out_specs=[pl.BlockSpec((n_batch, c, w), lambda t: (0, fwd_pos(t), 0)),
                   pl.BlockSpec((n_batch, c, w), lambda t: (0, bwd_pos(t), 0))],
        out_shape=[out_shape, out_shape],
        scratch_shapes=[pltpu.VMEM((n_batch, 2, C_HEADS, HEAD_DIM, HEAD_DIM), F32)],
        compiler_params=_params("arbitrary"),
        name="retention",
    )(*args)


def _gn_gate_kernel(of_ref, ob_ref, grs_ref, g_ref, o_ref):
    for h in range(C_HEADS):
        cols = slice(h * HEAD_DIM, (h + 1) * HEAD_DIM)
        o = of_ref[:, cols] + ob_ref[:, cols]
        mu = jnp.mean(o, axis=-1, keepdims=True)
        var = jnp.mean(jnp.square(o - mu), axis=-1, keepdims=True)
        y = (o - mu) * lax.rsqrt(var + GN_EPS) * g_ref[:, cols]
        o_ref[:, cols] = (grs_ref[:, cols] * y).astype(o_ref.dtype)


def _gn_gate(o_f, o_b, grs, gn_g, layer, m, *, lat_rows, seq, ctx_len):
    w = o_f.shape[-1]
    tm = _tile(ctx_len, 256, 8)

    def scan_block(i):
        r = i * tm
        in_lat = r < lat_rows
        b = jnp.where(in_lat, r // seq, (r - lat_rows) // ctx_len)
        off = jnp.where(in_lat, ctx_len + r % seq, (r - lat_rows) % ctx_len)
        return b, off // tm, 0

    row = pl.BlockSpec((tm, w), lambda i: (i, 0))
    scan = pl.BlockSpec((None, tm, w), scan_block)
    return pl.pallas_call(
        _gn_gate_kernel,
        grid=(m // tm,),
        in_specs=[scan, scan, row, pl.BlockSpec((None, 1, w), lambda i: (layer, 0, 0))],
        out_specs=row,
        out_shape=jax.ShapeDtypeStruct((m, w), BF16),
        compiler_params=_params("parallel"),
        name="gn_gate",
    )(o_f, o_b, grs, gn_g)


def _merge_kernel(ya_ref, yb_ref, yc_ref, w_ref, ga_ref, gb_ref, gc_ref, o_ref, *wbs, kc):
    j, i = pl.program_id(0), pl.program_id(1)

    def stage(s):
        wbs[s][:, pl.ds(pl.multiple_of(i * kc, kc), kc), :] = w_ref[...].astype(BF16)

    def compute(s):
        out = None
        for br, (y_ref, g_ref) in enumerate(((ya_ref, ga_ref), (yb_ref, gb_ref), (yc_ref, gc_ref))):
            t = g_ref[...].astype(F32) * jnp.dot(y_ref[...], wbs[s][br], preferred_element_type=F32)
            out = t if out is None else out + t
        o_ref[...] = out.astype(o_ref.dtype)

    _staged_phases(j, stage, compute)


def _merge(y_a, y_b, y_c, gates, w_branch, layer, m):
    d = w_branch.shape[-1]
    bw = w_branch.shape[-2]
    ni = _MM_ROW_TILES
    tm, kc = m // ni, bw // ni
    assert m % ni == 0 and tm % 8 == 0 and bw % ni == 0 and kc % 16 == 0
    tn = _tile(d, 1024, 128)
    nj = d // tn
    row = lambda j, i: jnp.where(j == 0, 0, i)
    col = lambda j: jnp.maximum(j - 1, 0)
    y_spec = pl.BlockSpec((tm, bw), lambda j, i: (row(j, i), 0))
    g_spec = lambda br: pl.BlockSpec((tm, tn), lambda j, i: (row(j, i), br * nj + col(j)))
    return pl.pallas_call(
        functools.partial(_merge_kernel, kc=kc),
        grid=(nj + 1, ni),
        in_specs=[y_spec, y_spec, y_spec,
                  pl.BlockSpec((None, N_BRANCH, kc, tn),
                               lambda j, i: (layer, 0, jnp.where(j == nj, ni - 1, i), jnp.minimum(j, nj - 1))),
                  g_spec(0), g_spec(1), g_spec(2)],
        out_specs=pl.BlockSpec((tm, tn), lambda j, i: (row(j, i), col(j))),
        out_shape=jax.ShapeDtypeStruct((m, d), BF16),
        scratch_shapes=[pltpu.VMEM((N_BRANCH, bw, tn), BF16)] * 2,
        compiler_params=_params("arbitrary", "arbitrary"),
        name="merge",
    )(y_a, y_b, y_c, w_branch, gates, gates, gates)


def _ln_kernel(z_ref, lng_ref, lnb_ref, *rest, emit_h):
    z = z_ref[...]
    mu = jnp.mean(z, axis=-1, keepdims=True)
    var = jnp.mean(jnp.square(z - mu), axis=-1, keepdims=True)
    out = (z - mu) * lax.rsqrt(var + LN_EPS) * lng_ref[...] + lnb_ref[...]
    if emit_h:
        sc_ref, sh_ref, lat_out, h_out = rest
        h_out[...] = (out * (1.0 + sc_ref[...]) + sh_ref[...]).astype(h_out.dtype)
    else:
        lat_out, = rest
    lat_out[...] = out


def _post_ln(z, mod_next, k_scale, k_shift, ln_g, ln_b, layer, m, group_of, group_rows):
    d = z.shape[1]
    tm = _tile(math.gcd(m, group_rows), 512, 256)
    emit_h = mod_next is not None
    row = pl.BlockSpec((tm, d), lambda i: (i, 0))
    vec = lambda k: pl.BlockSpec((None, None, 1, d), lambda i: (k, group_of(i * tm), 0, 0))
    par = pl.BlockSpec((None, 1, d), lambda i: (layer, 0, 0))
    in_specs = [row, par, par]
    args = [z, ln_g, ln_b]
    out_specs = [row]
    out_shape = [jax.ShapeDtypeStruct((m, d), F32)]
    if emit_h:
        in_specs += [vec(k_scale), vec(k_shift)]
        args += [mod_next, mod_next]
        out_specs.append(row)
        out_shape.append(jax.ShapeDtypeStruct((m, d), BF16))
    res = pl.pallas_call(
        functools.partial(_ln_kernel, emit_h=emit_h),
        grid=(m // tm,),
        in_specs=in_specs,
        out_specs=out_specs,
        out_shape=out_shape,
        compiler_params=_params("parallel"),
        name="post_ln",
    )(*args)
    return (res[0], res[1]) if emit_h else (res[0], None)


_HALO = 8
_UP_ROW_TILES = 4


def _up_conv_kernel(a_ref, wg_ref, wu_ref, cw_ref, cb_ref, wd_ref, o_ref, wd_out_ref, *wbs,
                    kc, tm, n_sub, seq_first_rows, seq_last_rows):
    j, i = pl.program_id(0), pl.program_id(1)

    def stage(s):
        wd_out_ref[...] = wd_ref[...].astype(BF16)
        rows = pl.ds(pl.multiple_of(i * kc, kc), kc)
        wbs[s][0, rows, :] = wg_ref[...].astype(BF16)
        wbs[s][1, rows, :] = wu_ref[...].astype(BF16)

    def compute(s):
        ts = tm // n_sub
        n_rows = ts + 2 * _HALO
        for k in range(n_sub):
            r0 = k * ts
            g = jnp.dot(a_ref[r0:r0 + n_rows, :], wbs[s][0], preferred_element_type=F32)
            u = jnp.dot(a_ref[r0 + _HALO:r0 + _HALO + ts, :], wbs[s][1], preferred_element_type=F32)
            g_prev = pltpu.roll(g, 1, 0)[_HALO:_HALO + ts]
            g_next = pltpu.roll(g, n_rows - 1, 0)[_HALO:_HALO + ts]
            row = i * tm + r0 + lax.broadcasted_iota(jnp.int32, (ts, 1), 0)
            has_prev = functools.reduce(lambda x, y: x & y, [row != r for r in seq_first_rows])
            has_next = functools.reduce(lambda x, y: x & y, [row != r for r in seq_last_rows])
            conv = (cb_ref[...] + jnp.where(has_prev, g_prev, 0.0) * cw_ref[0:1, :]
                    + g[_HALO:_HALO + ts] * cw_ref[1:2, :] + jnp.where(has_next, g_next, 0.0) * cw_ref[2:3, :])
            o_ref[r0:r0 + ts, :] = (_silu(conv) * u).astype(o_ref.dtype)

    _staged_phases(j, stage, compute)


def _up_conv_gate(h, w_up, conv_w, conv_b, w_down, layer, m, seq_lens):
    k = h.shape[1]
    f = w_up.shape[-1] // 2
    d_out = w_down.shape[-1]
    ni = _UP_ROW_TILES
    tm, kc = m // ni, k // ni
    assert m % ni == 0 and tm % 8 == 0 and k % ni == 0 and kc % 16 == 0 and sum(seq_lens) == m
    tf = _tile(f, 256, 128)
    nj = f // tf
    wd_rows = f // (nj * ni)
    assert f % (nj * ni) == 0 and wd_rows % 16 == 0
    wd_blk = lambda j, i: jnp.maximum(j - 1, 0) * ni + jnp.where(j == 0, 0, i)
    starts = [sum(seq_lens[:s]) for s in range(len(seq_lens))]
    first_rows = tuple(starts)
    last_rows = tuple(st + n - 1 for st, n in zip(starts, seq_lens))
    edge = jnp.zeros((_HALO, k), h.dtype)
    pieces = []
    for t in range(ni):
        lo, hi = t * tm, (t + 1) * tm
        pieces += [h[lo - _HALO:lo] if t else edge, h[lo:hi], h[hi:hi + _HALO] if hi < m else edge]
    tiles = jnp.concatenate(pieces, axis=0).reshape(ni, tm + 2 * _HALO, k)
    row = lambda j, i: jnp.where(j == 0, 0, i)
    col = lambda j: jnp.maximum(j - 1, 0)
    chunk = lambda j, i: jnp.where(j == nj, ni - 1, i)
    stage = lambda j: jnp.minimum(j, nj - 1)
    n_sub = max(1, tm // 512)
    assert tm % (16 * n_sub) == 0
    body = functools.partial(_up_conv_kernel, kc=kc, tm=tm, n_sub=n_sub,
                             seq_first_rows=first_rows, seq_last_rows=last_rows)
    return pl.pallas_call(
        body,
        grid=(nj + 1, ni),
        in_specs=[pl.BlockSpec((None, tm + 2 * _HALO, k), lambda j, i: (row(j, i), 0, 0)),
                  pl.BlockSpec((None, kc, tf), lambda j, i: (layer, chunk(j, i), stage(j))),
                  pl.BlockSpec((None, kc, tf), lambda j, i: (layer, chunk(j, i), nj + stage(j))),
                  pl.BlockSpec((None, CONV_WIDTH, tf), lambda j, i: (layer, 0, col(j))),
                  pl.BlockSpec((None, 1, tf), lambda j, i: (layer, 0, col(j))),
                  pl.BlockSpec((None, wd_rows, d_out), lambda j, i: (layer, wd_blk(j, i), 0))],
        out_specs=[pl.BlockSpec((tm, tf), lambda j, i: (row(j, i), col(j))),
                   pl.BlockSpec((wd_rows, d_out), lambda j, i: (wd_blk(j, i), 0))],
        out_shape=[jax.ShapeDtypeStruct((m, f), BF16), jax.ShapeDtypeStruct((f, d_out), BF16)],
        scratch_shapes=[pltpu.VMEM((2, k, tf), BF16)] * 2,
        compiler_params=_params("arbitrary", "arbitrary"),
        name="up_conv_gate",
    )(tiles, w_up, w_up, conv_w, conv_b, w_down)


def _rope_tables(seq, n_batch, n_ctx_rows):
    rows = seq // GRID_W
    r_idx, c_idx = jnp.meshgrid(jnp.arange(rows, dtype=F32), jnp.arange(GRID_W, dtype=F32), indexing='ij')
    axis_dim = HEAD_DIM // 2
    inv_freq = ROPE_BASE ** (-jnp.arange(0, axis_dim, 2, dtype=F32) / axis_dim)
    ang_r = r_idx.reshape(-1)[:, None] * inv_freq
    ang_c = c_idx.reshape(-1)[:, None] * inv_freq
    cr, sr, cc, sn = jnp.cos(ang_r), jnp.sin(ang_r), jnp.cos(ang_c), jnp.sin(ang_c)
    zero = jnp.zeros_like(sr)
    c = jnp.concatenate([cr, cr, cc, cc], axis=-1)
    sa = jnp.concatenate([-sr, zero, -sn, zero], axis=-1)
    sb = jnp.concatenate([zero, sr, zero, sn], axis=-1)

    def full(t, fill):
        return jnp.concatenate([jnp.tile(t, (n_batch, 1)), jnp.full((n_ctx_rows, HEAD_DIM), fill, F32)], axis=0)

    return full(c, 1.0), full(sa, 0.0), full(sb, 0.0)


def kernel(x, c, ctx, c_ctx, w_mod, b_mod, w_in, b_gate, w_branch, w_o, attn_sink, q_norm_g, k_norm_g,
           ret_decay, ret_gn_g, ln1_g, ln1_b, w_up, ffn_conv_w, ffn_conv_b, w_down, ln2_g, ln2_b):
    n_batch, seq, d = x.shape
    ctx_len = ctx.shape[1]
    depth = w_mod.shape[0]
    d_ff = w_down.shape[1]
    lat_rows = n_batch * seq
    m_all = lat_rows + n_batch * ctx_len
    alpha = (2 * depth) ** 0.25
    qkv_w = _N_HEADS_QKV * HEAD_DIM
    assert seq % 256 == 0 and ctx_len % 256 == 0 and lat_rows % ctx_len == 0

    def group_of(row_start):
        return jnp.minimum(row_start // seq, n_batch)

    group_bounds = [seq * (b + 1) for b in range(n_batch)]

    c_rows = jnp.concatenate([c, c_ctx[None], jnp.zeros((8 - n_batch - 1, d), F32)], axis=0)
    mod = _modulation(c_rows, w_mod, b_mod)
    mod = mod.reshape(depth, 8, 6, d).transpose(0, 2, 1, 3)[:, :, :, None, :]
    SH1, SC1, G1, SH2, SC2, G2 = range(6)

    rope_c, rope_sa, rope_sb = _rope_tables(seq, n_batch, n_batch * ctx_len)
    log_gamma = -jnp.exp(ret_decay.astype(F32))
    b_gate3 = b_gate.reshape(depth, 1, -1)
    gn_g3 = ret_gn_g.reshape(depth, 1, -1)
    ln1_g3, ln1_b3 = ln1_g.reshape(depth, 1, d), ln1_b.reshape(depth, 1, d)
    ln2_g3, ln2_b3 = ln2_g.reshape(depth, 1, d), ln2_b.reshape(depth, 1, d)
    conv_b3 = ffn_conv_b.reshape(depth, 1, d_ff)

    lat, h = _modulate(x.reshape(lat_rows, d), ctx.reshape(n_batch * ctx_len, d), mod[0], SC1, SH1, group_of)

    for l in range(depth):
        last = l == depth - 1
        m = lat_rows if last else m_all

        p = _matmul(h, w_in, l, 0, qkv_w, m_all, F32, name="in_proj_qkv")
        gates = _matmul(h, w_in, l, qkv_w, N_BRANCH * d, m, BF16, bias=b_gate3, name="in_proj_gate")
        act, krf, grs = _prep(p, rope_c, rope_sa, rope_sb, q_norm_g[l][None], k_norm_g[l][None])

        tq = ATTN_TQ
        r128 = tq // 128
        lat128 = seq // 128
        ctx_seg = lambda k0, v0: (ctx_len, lambda b, n: lat_rows // ctx_len + b, k0, v0, None)
        segs_a = [
            (128, lambda b, n: b * lat128 + jnp.maximum(n * r128 - 1, 0), _KA, _VA, -128),
            (tq, lambda b, n: b * (seq // tq) + n, _KA, _VA, 0),
            (128, lambda b, n: b * lat128 + jnp.minimum((n + 1) * r128, lat128 - 1), _KA, _VA, tq),
            ctx_seg(_KA, _VA),
        ]
        attn = functools.partial(_attention, act, n_batch=n_batch, out_rows=m)
        y_a = attn(segs_a, q_rows0=0, q_len=seq, q_head0=_QA, n_kv=A_KV_HEADS,
                   group=A_HEADS // A_KV_HEADS, tq=tq, sink=attn_sink[l], name="attn_window")
        segs_b = [ctx_seg(_KB, _VB), (seq, lambda b, n: b, _KB, _VB, None)]
        y_b = attn(segs_b, q_rows0=0, q_len=seq, q_head0=_QB, n_kv=B_KV_HEADS,
                   group=B_HEADS // B_KV_HEADS, tq=ATTN_TQ, sink=None, name="attn_dense", online=True)
        if not last:
            y_a = attn([ctx_seg(_KA, _VA)], q_rows0=lat_rows, q_len=ctx_len, q_head0=_QA,
                       n_kv=A_KV_HEADS, group=A_HEADS // A_KV_HEADS, tq=ctx_len, sink=attn_sink[l],
                       name="attn_ctx_a", into=y_a)
            y_b = attn([ctx_seg(_KB, _VB)], q_rows0=lat_rows, q_len=ctx_len, q_head0=_QB,
                       n_kv=B_KV_HEADS, group=B_HEADS // B_KV_HEADS, tq=ctx_len, sink=None,
                       name="attn_ctx_b", into=y_b)

        o_f, o_b = _retention(act, krf, log_gamma[l], n_batch=n_batch, seq=seq, ctx_len=ctx_len)
        y_c = _gn_gate(o_f, o_b, grs, gn_g3, l, m, lat_rows=lat_rows, seq=seq, ctx_len=ctx_len)

        merged = _merge(y_a, y_b, y_c, gates, w_branch, l, m)
        g1_rows = mod[l, G1].reshape(8, d)
        z = _matmul(merged, w_o, l, 0, d, m, F32, tn=512, name="out_proj",
                    resid=(lat, g1_rows, alpha, group_bounds))
        lat, h2 = _post_ln(z, mod[l], SC2, SH2, ln1_g3, ln1_b3, l, m, group_of, seq)

        seq_lens = [seq] * n_batch + ([] if last else [ctx_len] * n_batch)
        a, w_down_bf = _up_conv_gate(h2, w_up, ffn_conv_w, conv_b3, w_down, l, m, seq_lens)
        z2 = _matmul_a_resident(a, w_down_bf, m, lat, mod[l], G2, group_of, alpha, name="down_proj")
        nxt = None if last else mod[l + 1]
        lat, h = _post_ln(z2, nxt, SC1, SH1, ln2_g3, ln2_b3, l, m, group_of, seq)

    return lat.reshape(n_batch, seq, d)
```
